```python
import jax, jax.numpy as jnp
from jax import lax
import numpy as np

D_MODEL = 4096
BATCH = 4
SEQ = 4096
DEPTH = 1

D_MIX = D_MODEL
C_CONV = D_MIX // 2
C_ATTN = D_MIX - C_CONV
HEAD_DIM = 128
N_HEADS = C_ATTN // HEAD_DIM
CONV_K = 31
GRID_W = 64
NA_ROWS = 8
NA_COLS = 16
D_FF = 256 * ((8 * D_MODEL // 3 + 255) // 256)
EPS = 1e-6
PROJ_OUT = 2 * C_CONV + 3 * C_ATTN

kernel_name = "hybrid_conformer_conv_natten_block"


def rms_norm(x, g):
    x32 = x.astype(jnp.float32)
    y = x32 * lax.rsqrt(jnp.mean(x32 * x32, axis=-1, keepdims=True) + EPS)
    return (y * g.astype(jnp.float32)).astype(x.dtype)


def layer_norm(x, g, b):
    x32 = x.astype(jnp.float32)
    mu = jnp.mean(x32, axis=-1, keepdims=True)
    xc = x32 - mu
    var = jnp.mean(xc * xc, axis=-1, keepdims=True)
    y = xc * lax.rsqrt(var + EPS) * g.astype(jnp.float32) + b.astype(jnp.float32)
    return y.astype(x.dtype)


def swiglu(h, w_gate, w_up, w_down):
    gate = jnp.einsum('bsd,df->bsf', h, w_gate)
    up = jnp.einsum('bsd,df->bsf', h, w_up)
    return jnp.einsum('bsf,fd->bsd', jax.nn.silu(gate) * up, w_down)


def conformer_conv(a, g, conv_w, conv_b, ln_g, ln_b):
    u = a * jax.nn.sigmoid(g)
    kernel = conv_w[:, None, :]
    y = lax.conv_general_dilated(
        u, kernel.astype(u.dtype), window_strides=(1,),
        padding=[(CONV_K // 2, CONV_K // 2)],
        dimension_numbers=('NWC', 'WIO', 'NWC'),
        feature_group_count=C_CONV)
    y = y + conv_b
    return jax.nn.silu(layer_norm(y, ln_g, ln_b))


def neighbourhood_attention(q, k, v, rpb):
    B, S, H, Dh = q.shape
    rows = S // GRID_W
    kh = min(NA_ROWS, rows)
    kw = NA_COLS
    n_cb = GRID_W // kw
    scale = Dh ** -0.5
    q_g = q.reshape(B, rows, GRID_W, H, Dh)
    k_g = k.reshape(B, rows, GRID_W, H, Dh)
    v_g = v.reshape(B, rows, GRID_W, H, Dh)

    r_all = jnp.arange(rows)
    row_start = jnp.clip(r_all - kh // 2, 0, rows - kh)
    key_rows = row_start[:, None] + jnp.arange(kh)[None, :]
    cb0 = jnp.arange(n_cb) * kw
    col_blk_start = jnp.clip(cb0 - kw // 2, 0, GRID_W - 2 * kw)
    key_cols = col_blk_start[:, None] + jnp.arange(2 * kw)[None, :]
    q_cols = jnp.arange(GRID_W).reshape(n_cb, kw)
    win_start = jnp.clip(q_cols - kw // 2, 0, GRID_W - kw)
    kc = key_cols[:, None, :]
    col_mask = (kc >= win_start[..., None]) & (kc < win_start[..., None] + kw)
    dc_idx = jnp.clip(kc - q_cols[..., None] + NA_COLS - 1, 0, 2 * NA_COLS - 2)
    mask = col_mask[None, None, :, :, None, :]

    def one_row(r):
        q_r = q_g[:, r].reshape(B, n_cb, kw, H, Dh)
        kr = key_rows[r]
        k_blk = jnp.take(jnp.take(k_g, kr, axis=1), key_cols, axis=2)
        v_blk = jnp.take(jnp.take(v_g, kr, axis=1), key_cols, axis=2)
        s = jnp.einsum('bcqhd,bkcjhd->bhcqkj', q_r, k_blk,
                       preferred_element_type=jnp.float32) * scale
        dr_idx = kr - r + NA_ROWS - 1
        bias = rpb[:, dr_idx, :][:, :, dc_idx]
        bias = jnp.transpose(bias, (0, 2, 3, 1, 4)).astype(jnp.float32)
        s = jnp.where(mask, s + bias[None], -jnp.inf)
        p = jax.nn.softmax(s.reshape(B, H, n_cb, kw, kh * 2 * kw), axis=-1)
        p = p.reshape(B, H, n_cb, kw, kh, 2 * kw).astype(v.dtype)
        return jnp.einsum('bhcqkj,bkcjhd->bcqhd', p, v_blk)

    out = lax.map(one_row, r_all)
    out = jnp.transpose(out, (1, 0, 2, 3, 4, 5))
    return out.reshape(B, S, H * Dh)


def setup_inputs(seed: int = 0) -> dict:
    key = jax.random.key(seed)
    ks = jax.random.split(key, 24)

    def nrm(k, shape, scale):
        return jax.random.normal(k, shape, jnp.float32) * scale

    def gain(k, shape):
        return 1.0 + 0.01 * jax.random.normal(k, shape, jnp.float32)

    L = DEPTH
    return {
        "x": nrm(ks[0], (BATCH, SEQ, D_MODEL), 1.0),
        "g_ffn1": gain(ks[1], (L, D_MODEL)),
        "w1_gate": nrm(ks[2], (L, D_MODEL, D_FF), D_MODEL ** -0.5),
        "w1_up": nrm(ks[3], (L, D_MODEL, D_FF), D_MODEL ** -0.5),
        "w1_down": nrm(ks[4], (L, D_FF, D_MODEL), D_FF ** -0.5),
        "g_mix": gain(ks[5], (L, D_MODEL)),
        "w_in": nrm(ks[6], (L, D_MODEL, PROJ_OUT), D_MODEL ** -0.5),
        "conv_w": nrm(ks[7], (L, CONV_K, C_CONV), CONV_K ** -0.5),
        "conv_b": nrm(ks[8], (L, C_CONV), 0.02),
        "conv_ln_g": gain(ks[9], (L, C_CONV)),
        "conv_ln_b": nrm(ks[10], (L, C_CONV), 0.02),
        "q_norm_g": gain(ks[11], (L, HEAD_DIM)),
        "k_norm_g": gain(ks[12], (L, HEAD_DIM)),
        "rpb": nrm(ks[13], (L, N_HEADS, 2 * NA_ROWS - 1, 2 * NA_COLS - 1), 0.1),
        "w_out": nrm(ks[14], (L, D_MIX, D_MODEL), D_MIX ** -0.5),
        "g_ffn2": gain(ks[15], (L, D_MODEL)),
        "w2_gate": nrm(ks[16], (L, D_MODEL, D_FF), D_MODEL ** -0.5),
        "w2_up": nrm(ks[17], (L, D_MODEL, D_FF), D_MODEL ** -0.5),
        "w2_down": nrm(ks[18], (L, D_FF, D_MODEL), D_FF ** -0.5),
        "g_final": gain(ks[19], (L, D_MODEL)),
    }


def reference(x, g_ffn1, w1_gate, w1_up, w1_down, g_mix, w_in, conv_w, conv_b,
              conv_ln_g, conv_ln_b, q_norm_g, k_norm_g, rpb, w_out, g_ffn2,
              w2_gate, w2_up, w2_down, g_final):
    B, S, _ = x.shape
    for l in range(DEPTH):
        x = x + 0.5 * swiglu(rms_norm(x, g_ffn1[l]), w1_gate[l], w1_up[l], w1_down[l])

        h = rms_norm(x, g_mix[l])
        proj = jnp.einsum('bsd,df->bsf', h, w_in[l])
        a, g, q, k, v = jnp.split(
            proj, [C_CONV, 2 * C_CONV, 2 * C_CONV + C_ATTN, 2 * C_CONV + 2 * C_ATTN], axis=-1)

        conv_out = conformer_conv(a, g, conv_w[l], conv_b[l], conv_ln_g[l], conv_ln_b[l])

        q = rms_norm(q.reshape(B, S, N_HEADS, HEAD_DIM), q_norm_g[l])
        k = rms_norm(k.reshape(B, S, N_HEADS, HEAD_DIM), k_norm_g[l])
        v = v.reshape(B, S, N_HEADS, HEAD_DIM)
        attn_out = neighbourhood_attention(q, k, v, rpb[l])

        mixed = jnp.concatenate([conv_out, attn_out.astype(conv_out.dtype)], axis=-1)
        x = x + jnp.einsum('bsf,fd->bsd', mixed, w_out[l])

        x = x + 0.5 * swiglu(rms_norm(x, g_ffn2[l]), w2_gate[l], w2_up[l], w2_down[l])

        x = rms_norm(x, g_final[l])
    return x
```

```python
import functools

import jax
import jax.numpy as jnp
from jax import lax
from jax.experimental import pallas as pl
from jax.experimental.pallas import tpu as pltpu

F32 = jnp.float32
BF16 = jnp.bfloat16

EPS = 1e-6
HEAD_DIM = 128
CONV_K = 31
GRID_W = 64
NA_ROWS = 8
NA_COLS = 16
Q_ROWS = 4
KEY_ROWS = Q_ROWS + NA_ROWS
MASK_VALUE = -1e30

V7X_VMEM_LIMIT = 56 * 1024 * 1024


def _block(dim, pref):
    b = min(dim, pref)
    while dim % b:
        b //= 2
    return b


def _params(semantics):
    return pltpu.CompilerParams(dimension_semantics=semantics, vmem_limit_bytes=V7X_VMEM_LIMIT)


def _rmsnorm_kernel(x_ref, g_ref, o_ref):
    x = x_ref[...]
    ms = jnp.mean(x * x, axis=-1, keepdims=True)
    o_ref[...] = ((x * lax.rsqrt(ms + EPS)) * g_ref[...]).astype(o_ref.dtype)


def _rmsnorm(x, g, out_dtype):
    m, d = x.shape
    bm = _block(m, 256)
    return pl.pallas_call(
        _rmsnorm_kernel,
        grid=(m // bm,),
        in_specs=[pl.BlockSpec((bm, d), lambda i: (i, 0)),
                  pl.BlockSpec((1, d), lambda i: (0, 0))],
        out_specs=pl.BlockSpec((bm, d), lambda i: (i, 0)),
        out_shape=jax.ShapeDtypeStruct((m, d), out_dtype),
        compiler_params=_params(("parallel",)),
        name="rmsnorm",
    )(x, g.reshape(1, d))


def _ffn_kernel(x_hbm, h_ref, wg_ref, wu_ref, wd_ref, gfin_ref, out_hbm, acc_ref, sem, *, bm, bn, final_norm):
    i = pl.program_id(0)
    f = pl.program_id(1)
    rows = pl.ds(pl.multiple_of(i * bm, bm), bm)
    d = acc_ref.shape[1]

    @pl.when(f == 0)
    def _():
        cp = pltpu.make_async_copy(x_hbm.at[rows], acc_ref, sem.at[0])
        cp.start()
        cp.wait()

    h = h_ref[...]
    gate = jnp.dot(h, wg_ref[...], preferred_element_type=F32)
    up = jnp.dot(h, wu_ref[...], preferred_element_type=F32)
    act = ((0.5 * (gate * jax.nn.sigmoid(gate))) * up).astype(BF16)
    for c in range(d // bn):
        cols = slice(c * bn, (c + 1) * bn)
        acc_ref[:, cols] += jnp.dot(act, wd_ref[:, cols], preferred_element_type=F32)

    @pl.when(f == pl.num_programs(1) - 1)
    def _():
        if final_norm:
            rc = min(bm, 64)

            def body(r, carry):
                rs = pl.ds(pl.multiple_of(r * rc, rc), rc)
                v = acc_ref[rs, :]
                ms = jnp.mean(v * v, axis=-1, keepdims=True)
                acc_ref[rs, :] = (v * lax.rsqrt(ms + EPS)) * gfin_ref[...]
                return carry

            lax.fori_loop(0, bm // rc, body, 0)
        cp = pltpu.make_async_copy(acc_ref, out_hbm.at[rows], sem.at[1])
        cp.start()
        cp.wait()


def _ffn(x, h, wg, wu, wd, g_final=None):
    m, d = x.shape
    ff = wg.shape[1]
    bm = _block(m, 1024)
    bf = _block(ff, 256)
    bn = _block(d, 512)
    final_norm = g_final is not None
    gfin = (g_final if final_norm else jnp.ones((d,), F32)).reshape(1, d)
    kern = functools.partial(_ffn_kernel, bm=bm, bn=bn, final_norm=final_norm)
    return pl.pallas_call(
        kern,
        grid=(m // bm, ff // bf),
        in_specs=[pl.BlockSpec(memory_space=pl.ANY),
                  pl.BlockSpec((bm, d), lambda i, f: (i, 0)),
                  pl.BlockSpec((d, bf), lambda i, f: (0, f)),
                  pl.BlockSpec((d, bf), lambda i, f: (0, f)),
                  pl.BlockSpec((bf, d), lambda i, f: (f, 0)),
                  pl.BlockSpec((1, d), lambda i, f: (0, 0))],
        out_specs=pl.BlockSpec(memory_space=pl.ANY),
        out_shape=jax.ShapeDtypeStruct((m, d), F32),
        scratch_shapes=[pltpu.VMEM((bm, d), F32), pltpu.SemaphoreType.DMA((2,))],
        compiler_params=_params(("arbitrary", "arbitrary")),
        name="ffn_final" if final_norm else "ffn",
    )(x, h, wg, wu, wd, gfin)


def _proj_glu_kernel(h_ref, wa_ref, wg_ref, o_ref):
    h = h_ref[...]
    a = jnp.dot(h, wa_ref[...], preferred_element_type=F32)
    g = jnp.dot(h, wg_ref[...], preferred_element_type=F32)
    o_ref[...] = a * jax.nn.sigmoid(g)


def _proj_glu(h, w_in, c_conv):
    m, d = h.shape
    bm = _block(m, 1024)
    bn = _block(c_conv, 512)
    nb = c_conv // bn
    return pl.pallas_call(
        _proj_glu_kernel,
        grid=(m // bm, nb),
        in_specs=[pl.BlockSpec((bm, d), lambda i, j: (i, 0)),
                  pl.BlockSpec((d, bn), lambda i, j: (0, j)),
                  pl.BlockSpec((d, bn), lambda i, j: (0, j + nb))],
        out_specs=pl.BlockSpec((bm, bn), lambda i, j: (i, j)),
        out_shape=jax.ShapeDtypeStruct((m, c_conv), F32),
        compiler_params=_params(("parallel", "arbitrary")),
        name="proj_glu",
    )(h, w_in, w_in)


def _proj_qk_kernel(h_ref, w_ref, g_ref, o_ref):
    t = jnp.dot(h_ref[...], w_ref[...], preferred_element_type=F32)
    for hd in range(t.shape[1] // HEAD_DIM):
        cols = slice(hd * HEAD_DIM, (hd + 1) * HEAD_DIM)
        th = t[:, cols]
        ms = jnp.mean(th * th, axis=-1, keepdims=True)
        o_ref[:, cols] = ((th * lax.rsqrt(ms + EPS)) * g_ref[:, cols]).astype(o_ref.dtype)


def _proj_qk(h, w_in, gains, col0):
    m, d = h.shape
    n = gains.shape[0]
    bm = _block(m, 1024)
    bn = _block(n, 512)
    off = col0 // bn
    return pl.pallas_call(
        _proj_qk_kernel,
        grid=(m // bm, n // bn),
        in_specs=[pl.BlockSpec((bm, d), lambda i, j: (i, 0)),
                  pl.BlockSpec((d, bn), lambda i, j: (0, j + off)),
                  pl.BlockSpec((1, bn), lambda i, j: (0, j))],
        out_specs=pl.BlockSpec((bm, bn), lambda i, j: (i, j)),
        out_shape=jax.ShapeDtypeStruct((m, n), BF16),
        compiler_params=_params(("parallel", "arbitrary")),
        name="proj_qk",
    )(h, w_in, gains.reshape(1, n))


def _proj_v_kernel(h_ref, w_ref, o_ref):
    o_ref[...] = jnp.dot(h_ref[...], w_ref[...], preferred_element_type=F32).astype(o_ref.dtype)


def _proj_v(h, w_in, col0, n):
    m, d = h.shape
    bm = _block(m, 1024)
    bn = _block(n, 512)
    off = col0 // bn
    return pl.pallas_call(
        _proj_v_kernel,
        grid=(m // bm, n // bn),
        in_specs=[pl.BlockSpec((bm, d), lambda i, j: (i, 0)),
                  pl.BlockSpec((d, bn), lambda i, j: (0, j + off))],
        out_specs=pl.BlockSpec((bm, bn), lambda i, j: (i, j)),
        out_shape=jax.ShapeDtypeStruct((m, n), BF16),
        compiler_params=_params(("parallel", "arbitrary")),
        name="proj_v",
    )(h, w_in)


CONV_HALO = 16
CONV_LANES = 256


def _conv_kernel(prev_ref, main_ref, next_ref, w_ref, b_ref, lg_ref, lb_ref, o_ref, xpad_ref, y_ref):
    t = pl.program_id(1)
    ts, c = main_ref.shape
    prev = prev_ref[...]
    nxt = next_ref[...]
    xpad_ref[0:CONV_HALO, :] = jnp.where(t == 0, jnp.zeros_like(prev), prev)
    xpad_ref[CONV_HALO:CONV_HALO + ts, :] = main_ref[...]
    xpad_ref[CONV_HALO + ts:, :] = jnp.where(t == pl.num_programs(1) - 1, jnp.zeros_like(nxt), nxt)

    first = CONV_HALO - CONV_K // 2

    def chunk(ci, carry):
        lanes = pl.ds(pl.multiple_of(ci * CONV_LANES, CONV_LANES), CONV_LANES)
        acc = jnp.zeros((ts, CONV_LANES), F32)
        for k in range(CONV_K):
            acc = acc + w_ref[k:k + 1, lanes] * xpad_ref[first + k:first + k + ts, lanes]
        y_ref[:, lanes] = acc + b_ref[:, lanes]
        return carry

    lax.fori_loop(0, c // CONV_LANES, chunk, 0)

    y = y_ref[...]
    mu = jnp.mean(y, axis=-1, keepdims=True)
    yc = y - mu
    var = jnp.mean(yc * yc, axis=-1, keepdims=True)
    z = (yc * lax.rsqrt(var + EPS)) * lg_ref[...] + lb_ref[...]
    o_ref[...] = (z * jax.nn.sigmoid(z)).astype(o_ref.dtype)


def _conv_module(u, seq, conv_w, conv_b, ln_g, ln_b):
    m, c = u.shape
    ts = _block(seq, 64)
    nt = seq // ts
    hb = ts // CONV_HALO
    nh = m // CONV_HALO
    last_h = seq // CONV_HALO - 1

    def prev_map(b, t):
        return (b * (seq // CONV_HALO) + jnp.maximum(t * hb - 1, 0), 0)

    def next_map(b, t):
        return (b * (seq // CONV_HALO) + jnp.minimum((t + 1) * hb, last_h), 0)

    del nh
    vec = lambda: pl.BlockSpec((1, c), lambda b, t: (0, 0))
    return pl.pallas_call(
        _conv_kernel,
        grid=(m // seq, nt),
        in_specs=[pl.BlockSpec((CONV_HALO, c), prev_map),
                  pl.BlockSpec((ts, c), lambda b, t: (b * nt + t, 0)),
                  pl.BlockSpec((CONV_HALO, c), next_map),
                  pl.BlockSpec((CONV_K, c), lambda b, t: (0, 0)),
                  vec(), vec(), vec()],
        out_specs=pl.BlockSpec((ts, c), lambda b, t: (b * nt + t, 0)),
        out_shape=jax.ShapeDtypeStruct((m, c), BF16),
        scratch_shapes=[pltpu.VMEM((ts + 2 * CONV_HALO, c), F32), pltpu.VMEM((ts, c), F32)],
        compiler_params=_params(("parallel", "arbitrary")),
        name="conv_module",
    )(u, u, u, conv_w, conv_b.reshape(1, c), ln_g.reshape(1, c), ln_b.reshape(1, c))


def _attn_bias_table(rpb, rows):
    tables = []
    qc = jnp.arange(GRID_W)
    kc = jnp.arange(GRID_W)
    win_start = jnp.clip(qc - NA_COLS // 2, 0, GRID_W - NA_COLS)
    col_ok = (kc[None, :] >= win_start[:, None]) & (kc[None, :] < win_start[:, None] + NA_COLS)
    dc = jnp.clip(kc[None, :] - qc[:, None] + NA_COLS - 1, 0, 2 * NA_COLS - 2)
    g = jnp.arange(Q_ROWS)
    j = jnp.arange(KEY_ROWS)
    for r0 in (0, Q_ROWS, rows - Q_ROWS):
        ks = min(max(r0 - NA_ROWS // 2, 0), rows - KEY_ROWS)
        r = r0 + g
        row_start = jnp.clip(r - NA_ROWS // 2, 0, rows - NA_ROWS)
        kr = ks + j
        row_ok = (kr[None, :] >= row_start[:, None]) & (kr[None, :] < row_start[:, None] + NA_ROWS)
        dr = jnp.clip(kr[None, :] - r[:, None] + NA_ROWS - 1, 0, 2 * NA_ROWS - 2)
        bias = rpb[:, dr[:, None, :, None], dc[None, :, None, :]]
        ok = row_ok[:, None, :, None] & col_ok[None, :, None, :]
        tbl = jnp.where(ok[None], bias, MASK_VALUE)
        tables.append(tbl.reshape(rpb.shape[0], Q_ROWS * GRID_W, KEY_ROWS * GRID_W))
    return jnp.stack(tables, axis=1).astype(F32)


def _attn_kernel(q_ref, k_ref, v_ref, tbl_ref, o_ref, *, rows, heads):
    rb = pl.program_id(2)
    r0 = rb * Q_ROWS
    ks = jnp.clip(r0 - NA_ROWS // 2, 0, rows - KEY_ROWS)
    nk = KEY_ROWS * GRID_W
    keys = pl.ds(pl.multiple_of(ks * GRID_W, GRID_W), nk)
    for hd in range(heads):
        cols = slice(hd * HEAD_DIM, (hd + 1) * HEAD_DIM)
        q = q_ref[:, cols]
        k = k_ref[keys, cols]
        v = v_ref[keys, cols]
        s = lax.dot_general(q, k, (((1,), (1,)), ((), ())), preferred_element_type=F32)
        s = s + tbl_ref[hd, 0]
        mx = jnp.max(s, axis=-1, keepdims=True)
        p = jnp.exp(s - mx)
        p = p / jnp.sum(p, axis=-1, keepdims=True)
        o = jnp.dot(p.astype(v.dtype), v, preferred_element_type=F32)
        o_ref[:, cols] = o.astype(o_ref.dtype)


def _attention(qk, v, tbl, batch, seq):
    m, c = v.shape
    rows = seq // GRID_W
    nrb = rows // Q_ROWS
    heads = min(4, c // HEAD_DIM)
    bc = heads * HEAD_DIM
    ngrp = c // bc
    bq = Q_ROWS * GRID_W

    def variant(rb):
        return jnp.where(rb == 0, 0, jnp.where(rb == nrb - 1, 2, 1))

    kern = functools.partial(_attn_kernel, rows=rows, heads=heads)
    return pl.pallas_call(
        kern,
        grid=(batch, ngrp, nrb),
        in_specs=[pl.BlockSpec((bq, bc), lambda b, g, r: (b * nrb + r, g)),
                  pl.BlockSpec((seq, bc), lambda b, g, r: (b, ngrp + g)),
                  pl.BlockSpec((seq, bc), lambda b, g, r: (b, g)),
                  pl.BlockSpec((heads, 1, bq, KEY_ROWS * GRID_W), lambda b, g, r: (g, variant(r), 0, 0))],
        out_specs=pl.BlockSpec((bq, bc), lambda b, g, r: (b * nrb + r, g)),
        out_shape=jax.ShapeDtypeStruct((m, c), BF16),
        compiler_params=_params(("parallel", "parallel", "arbitrary")),
        name="natten",
    )(qk, qk, v, tbl)


def _out_proj_kernel(x_ref, c_ref, a_ref, wc_ref, wa_ref, o_ref):
    acc = jnp.dot(c_ref[...], wc_ref[...], preferred_element_type=F32)
    acc = acc + jnp.dot(a_ref[...], wa_ref[...], preferred_element_type=F32)
    o_ref[...] = x_ref[...] + acc


def _out_proj(x, conv_out, attn_out, w_out):
    m, d = x.shape
    cc = conv_out.shape[1]
    ca = attn_out.shape[1]
    assert cc == ca, "the two head groups share one weight array split at its row midpoint"
    bm = _block(m, 1024)
    bn = _block(d, 512)
    return pl.pallas_call(
        _out_proj_kernel,
        grid=(m // bm, d // bn),
        in_specs=[pl.BlockSpec((bm, bn), lambda i, j: (i, j)),
                  pl.BlockSpec((bm, cc), lambda i, j: (i, 0)),
                  pl.BlockSpec((bm, ca), lambda i, j: (i, 0)),
                  pl.BlockSpec((cc, bn), lambda i, j: (0, j)),
                  pl.BlockSpec((ca, bn), lambda i, j: (1, j))],
        out_specs=pl.BlockSpec((bm, bn), lambda i, j: (i, j)),
        out_shape=jax.ShapeDtypeStruct((m, d), F32),
        compiler_params=_params(("parallel", "arbitrary")),
        name="out_proj",
    )(x, conv_out, attn_out, w_out, w_out)


def kernel(x, g_ffn1, w1_gate, w1_up, w1_down, g_mix, w_in, conv_w, conv_b, conv_ln_g, conv_ln_b, q_norm_g, k_norm_g, rpb, w_out, g_ffn2, w2_gate, w2_up, w2_down, g_final):
    batch, seq, d = x.shape
    depth = g_ffn1.shape[0]
    c_conv = conv_w.shape[2]
    c_attn = w_out.shape[1] - c_conv
    n_heads = c_attn // HEAD_DIM
    rows = seq // GRID_W
    scale = HEAD_DIM ** -0.5

    xs = x.reshape(batch * seq, d)
    for l in range(depth):
        wg1, wu1, wd1 = w1_gate[l].astype(BF16), w1_up[l].astype(BF16), w1_down[l].astype(BF16)
        wg2, wu2, wd2 = w2_gate[l].astype(BF16), w2_up[l].astype(BF16), w2_down[l].astype(BF16)
        w_in_b = w_in[l].astype(BF16)
        w_out_b = w_out[l].astype(BF16)
        qk_gains = jnp.concatenate([jnp.tile(q_norm_g[l] * scale, n_heads), jnp.tile(k_norm_g[l], n_heads)])
        tbl = _attn_bias_table(rpb[l], rows)

        h1 = _rmsnorm(xs, g_ffn1[l], BF16)
        xs = _ffn(xs, h1, wg1, wu1, wd1)

        h2 = _rmsnorm(xs, g_mix[l], BF16)
        u = _proj_glu(h2, w_in_b, c_conv)
        qk = _proj_qk(h2, w_in_b, qk_gains, 2 * c_conv)
        v = _proj_v(h2, w_in_b, 2 * c_conv + 2 * c_attn, c_attn)
        conv_out = _conv_module(u, seq, conv_w[l], conv_b[l], conv_ln_g[l], conv_ln_b[l])
        attn_out = _attention(qk, v, tbl, batch, seq)
        xs = _out_proj(xs, conv_out, attn_out, w_out_b)

        h3 = _rmsnorm(xs, g_ffn2[l], BF16)
        xs = _ffn(xs, h3, wg2, wu2, wd2, g_final=g_final[l])
    return xs.reshape(batch, seq, d)
```

```python
import functools

import jax
import jax.numpy as jnp
from jax import lax
from jax.experimental import pallas as pl
from jax.experimental.pallas import tpu as pltpu

F32 = jnp.float32
BF16 = jnp.bfloat16

EPS = 1e-6
HEAD_DIM = 128
CONV_K = 31
GRID_W = 64
NA_ROWS = 8
NA_COLS = 16
Q_ROWS = 4
KEY_ROWS = Q_ROWS + NA_ROWS
MASK_VALUE = -1e30

V7X_VMEM_LIMIT = 56 * 1024 * 1024


def _block(dim, pref):
    b = min(dim, pref)
    while dim % b:
        b //= 2
    return b


def _params(semantics):
    return pltpu.CompilerParams(dimension_semantics=semantics, vmem_limit_bytes=V7X_VMEM_LIMIT)


def _rmsnorm_kernel(x_ref, g_ref, o_ref):
    x = x_ref[...]
    ms = jnp.mean(x * x, axis=-1, keepdims=True)
    o_ref[...] = ((x * lax.rsqrt(ms + EPS)) * g_ref[...]).astype(o_ref.dtype)


def _rmsnorm(x, g, out_dtype):
    m, d = x.shape
    bm = _block(m, 256)
    return pl.pallas_call(
        _rmsnorm_kernel,
        grid=(m // bm,),
        in_specs=[pl.BlockSpec((bm, d), lambda i: (i, 0)),
                  pl.BlockSpec((1, d), lambda i: (0, 0))],
        out_specs=pl.BlockSpec((bm, d), lambda i: (i, 0)),
        out_shape=jax.ShapeDtypeStruct((m, d), out_dtype),
        compiler_params=_params(("parallel",)),
        name="rmsnorm",
    )(x, g.reshape(1, d))


FFN_NORM_ROWS = 64


def _ffn_kernel(x_hbm, h_ref, wg_ref, wu_ref, wd_ref, gfin_ref, out_hbm, acc_ref, sem_in, sem_out,
                *, bm, bn, nf, final_norm):
    i = pl.program_id(0)
    f = pl.program_id(1)
    rows = pl.ds(pl.multiple_of(i * bm, bm), bm)
    d = acc_ref.shape[1]
    nc = d // bn
    rc = min(bm, FFN_NORM_ROWS)

    def seed_copy(c):
        cols = pl.ds(c * bn, bn)
        return pltpu.make_async_copy(x_hbm.at[rows, cols], acc_ref.at[:, cols], sem_in.at[c])

    def col_writeback(c):
        cols = pl.ds(c * bn, bn)
        return pltpu.make_async_copy(acc_ref.at[:, cols], out_hbm.at[rows, cols], sem_out.at[c])

    def row_writeback(r):
        src = acc_ref.at[pl.ds(pl.multiple_of(r * rc, rc), rc)]
        dst = out_hbm.at[pl.ds(pl.multiple_of(i * bm + r * rc, rc), rc)]
        return pltpu.make_async_copy(src, dst, sem_out.at[r])

    def step(first, last):
        if first:
            for c in range(nc):
                seed_copy(c).start()
        h = h_ref[...]
        gate = jnp.dot(h, wg_ref[...], preferred_element_type=F32)
        up = jnp.dot(h, wu_ref[...], preferred_element_type=F32)
        act = ((0.5 * (gate * jax.nn.sigmoid(gate))) * up).astype(BF16)
        for c in range(nc):
            cols = slice(c * bn, (c + 1) * bn)
            if first:
                seed_copy(c).wait()
            acc_ref[:, cols] += jnp.dot(act, wd_ref[:, cols], preferred_element_type=F32)
            if last and not final_norm:
                col_writeback(c).start()
        if last and final_norm:
            def norm_rows(r, carry):
                rs = pl.ds(pl.multiple_of(r * rc, rc), rc)
                v = acc_ref[rs, :]
                ms = jnp.mean(v * v, axis=-1, keepdims=True)
                acc_ref[rs, :] = (v * lax.rsqrt(ms + EPS)) * gfin_ref[...]
                row_writeback(r).start()
                return carry

            lax.fori_loop(0, bm // rc, norm_rows, 0)

            def wait_rows(r, carry):
                row_writeback(r).wait()
                return carry

            lax.fori_loop(0, bm // rc, wait_rows, 0)
        elif last:
            for c in range(nc):
                col_writeback(c).wait()

    if nf == 1:
        step(True, True)
    else:
        pl.when(f == 0)(lambda: step(True, False))
        pl.when((f > 0) & (f < nf - 1))(lambda: step(False, False))
        pl.when(f == nf - 1)(lambda: step(False, True))


def _ffn(x, h, wg, wu, wd, g_final=None):
    m, d = x.shape
    ff = wg.shape[1]
    bm = _block(m, 1024)
    bf = _block(ff, 256)
    bn = _block(d, 512)
    nf = ff // bf
    final_norm = g_final is not None
    gfin = (g_final if final_norm else jnp.ones((d,), F32)).reshape(1, d)
    n_out_sems = max(d // bn, bm // min(bm, FFN_NORM_ROWS))
    kern = functools.partial(_ffn_kernel, bm=bm, bn=bn, nf=nf, final_norm=final_norm)
    return pl.pallas_call(
        kern,
        grid=(m // bm, ff // bf),
        in_specs=[pl.BlockSpec(memory_space=pl.ANY),
                  pl.BlockSpec((bm, d), lambda i, f: (i, 0)),
                  pl.BlockSpec((d, bf), lambda i, f: (0, f)),
                  pl.BlockSpec((d, bf), lambda i, f: (0, f)),
                  pl.BlockSpec((bf, d), lambda i, f: (f, 0)),
                  pl.BlockSpec((1, d), lambda i, f: (0, 0))],
        out_specs=pl.BlockSpec(memory_space=pl.ANY),
        out_shape=jax.ShapeDtypeStruct((m, d), F32),
        scratch_shapes=[pltpu.VMEM((bm, d), F32), pltpu.SemaphoreType.DMA((d // bn,)),
                        pltpu.SemaphoreType.DMA((n_out_sems,))],
        compiler_params=_params(("arbitrary", "arbitrary")),
        name="ffn_final" if final_norm else "ffn",
    )(x, h, wg, wu, wd, gfin)


def _proj_glu_kernel(h_ref, wa_ref, wg_ref, o_ref):
    h = h_ref[...]
    a = jnp.dot(h, wa_ref[...], preferred_element_type=F32)
    g = jnp.dot(h, wg_ref[...], preferred_element_type=F32)
    o_ref[...] = a * jax.nn.sigmoid(g)


def _proj_glu(h, w_in, c_conv):
    m, d = h.shape
    bm = _block(m, 1024)
    bn = _block(c_conv, 512)
    nb = c_conv // bn
    return pl.pallas_call(
        _proj_glu_kernel,
        grid=(m // bm, nb),
        in_specs=[pl.BlockSpec((bm, d), lambda i, j: (i, 0)),
                  pl.BlockSpec((d, bn), lambda i, j: (0, j)),
                  pl.BlockSpec((d, bn), lambda i, j: (0, j + nb))],
        out_specs=pl.BlockSpec((bm, bn), lambda i, j: (i, j)),
        out_shape=jax.ShapeDtypeStruct((m, c_conv), F32),
        compiler_params=_params(("parallel", "arbitrary")),
        name="proj_glu",
    )(h, w_in, w_in)


def _proj_qk_kernel(h_ref, w_ref, g_ref, o_ref):
    t = jnp.dot(h_ref[...], w_ref[...], preferred_element_type=F32)
    for hd in range(t.shape[1] // HEAD_DIM):
        cols = slice(hd * HEAD_DIM, (hd + 1) * HEAD_DIM)
        th = t[:, cols]
        ms = jnp.mean(th * th, axis=-1, keepdims=True)
        o_ref[:, cols] = ((th * lax.rsqrt(ms + EPS)) * g_ref[:, cols]).astype(o_ref.dtype)


def _proj_qk(h, w_in, gains, col0):
    m, d = h.shape
    n = gains.shape[0]
    bm = _block(m, 1024)
    bn = _block(n, 512)
    off = col0 // bn
    return pl.pallas_call(
        _proj_qk_kernel,
        grid=(m // bm, n // bn),
        in_specs=[pl.BlockSpec((bm, d), lambda i, j: (i, 0)),
                  pl.BlockSpec((d, bn), lambda i, j: (0, j + off)),
                  pl.BlockSpec((1, bn), lambda i, j: (0, j))],
        out_specs=pl.BlockSpec((bm, bn), lambda i, j: (i, j)),
        out_shape=jax.ShapeDtypeStruct((m, n), BF16),
        compiler_params=_params(("parallel", "arbitrary")),
        name="proj_qk",
    )(h, w_in, gains.reshape(1, n))


def _proj_v_kernel(h_ref, w_ref, o_ref):
    o_ref[...] = jnp.dot(h_ref[...], w_ref[...], preferred_element_type=F32).astype(o_ref.dtype)


def _proj_v(h, w_in, col0, n):
    m, d = h.shape
    bm = _block(m, 1024)
    bn = _block(n, 512)
    off = col0 // bn
    return pl.pallas_call(
        _proj_v_kernel,
        grid=(m // bm, n // bn),
        in_specs=[pl.BlockSpec((bm, d), lambda i, j: (i, 0)),
                  pl.BlockSpec((d, bn), lambda i, j: (0, j + off))],
        out_specs=pl.BlockSpec((bm, bn), lambda i, j: (i, j)),
        out_shape=jax.ShapeDtypeStruct((m, n), BF16),
        compiler_params=_params(("parallel", "arbitrary")),
        name="proj_v",
    )(h, w_in)


CONV_HALO = 16
CONV_LANES = 256


SUBLANES = 8
CONV_FIRST = CONV_HALO - CONV_K // 2
CONV_SHIFT_ROWS = SUBLANES * ((CONV_FIRST + CONV_K - 1) // SUBLANES)


def _conv_kernel(prev_ref, main_ref, next_ref, w_ref, b_ref, lg_ref, lb_ref, o_ref, xpad_ref, shift_ref, y_ref):
    t = pl.program_id(1)
    ts, c = main_ref.shape
    prev = prev_ref[...]
    nxt = next_ref[...]
    xpad_ref[0:CONV_HALO, :] = jnp.where(t == 0, jnp.zeros_like(prev), prev)
    xpad_ref[CONV_HALO:CONV_HALO + ts, :] = main_ref[...]
    xpad_ref[CONV_HALO + ts:, :] = jnp.where(t == pl.num_programs(1) - 1, jnp.zeros_like(nxt), nxt)
    n_shift = ts + CONV_SHIFT_ROWS

    def chunk(ci, carry):
        lanes = pl.ds(pl.multiple_of(ci * CONV_LANES, CONV_LANES), CONV_LANES)
        for s in range(1, SUBLANES):
            shift_ref[s - 1] = xpad_ref[s:s + n_shift, lanes]
        acc = jnp.zeros((ts, CONV_LANES), F32)
        for k in range(CONV_K):
            a, s = divmod(CONV_FIRST + k, SUBLANES)
            if s == 0:
                xk = xpad_ref[a * SUBLANES:a * SUBLANES + ts, lanes]
            else:
                xk = shift_ref[s - 1, a * SUBLANES:a * SUBLANES + ts, :]
            acc = acc + w_ref[k:k + 1, lanes] * xk
        y_ref[:, lanes] = acc + b_ref[:, lanes]
        return carry

    lax.fori_loop(0, c // CONV_LANES, chunk, 0)

    y = y_ref[...]
    mu = jnp.mean(y, axis=-1, keepdims=True)
    yc = y - mu
    var = jnp.mean(yc * yc, axis=-1, keepdims=True)
    z = (yc * lax.rsqrt(var + EPS)) * lg_ref[...] + lb_ref[...]
    o_ref[...] = (z * jax.nn.sigmoid(z)).astype(o_ref.dtype)


def _conv_module(u, seq, conv_w, conv_b, ln_g, ln_b):
    m, c = u.shape
    ts = _block(seq, 64)
    nt = seq // ts
    hb = ts // CONV_HALO
    nh = m // CONV_HALO
    last_h = seq // CONV_HALO - 1

    def prev_map(b, t):
        return (b * (seq // CONV_HALO) + jnp.maximum(t * hb - 1, 0), 0)

    def next_map(b, t):
        return (b * (seq // CONV_HALO) + jnp.minimum((t + 1) * hb, last_h), 0)

    del nh
    vec = lambda: pl.BlockSpec((1, c), lambda b, t: (0, 0))
    return pl.pallas_call(
        _conv_kernel,
        grid=(m // seq, nt),
        in_specs=[pl.BlockSpec((CONV_HALO, c), prev_map),
                  pl.BlockSpec((ts, c), lambda b, t: (b * nt + t, 0)),
                  pl.BlockSpec((CONV_HALO, c), next_map),
                  pl.BlockSpec((CONV_K, c), lambda b, t: (0, 0)),
                  vec(), vec(), vec()],
        out_specs=pl.BlockSpec((ts, c), lambda b, t: (b * nt + t, 0)),
        out_shape=jax.ShapeDtypeStruct((m, c), BF16),
        scratch_shapes=[pltpu.VMEM((ts + 2 * CONV_HALO, c), F32),
                        pltpu.VMEM((SUBLANES - 1, ts + CONV_SHIFT_ROWS, CONV_LANES), F32),
                        pltpu.VMEM((ts, c), F32)],
        compiler_params=_params(("parallel", "arbitrary")),
        name="conv_module",
    )(u, u, u, conv_w, conv_b.reshape(1, c), ln_g.reshape(1, c), ln_b.reshape(1, c))


N_DR = 2 * NA_ROWS - 1
N_DC = 2 * NA_COLS - 1


def _bias_tile_plan(rows):
    plan = {}
    for v, r0 in enumerate((0, Q_ROWS, rows - Q_ROWS)):
        ks = min(max(r0 - NA_ROWS // 2, 0), rows - KEY_ROWS)
        for g in range(Q_ROWS):
            r = r0 + g
            row_start = min(max(r - NA_ROWS // 2, 0), rows - NA_ROWS)
            for p in range(KEY_ROWS // 2):
                oks = [row_start <= ks + j < row_start + NA_ROWS for j in (2 * p, 2 * p + 1)]
                dr_left = ks + 2 * p - r + NA_ROWS - 1
                key = (dr_left if any(oks) else None, oks[0], oks[1])
                plan.setdefault(key, []).append((v, g, p))
    return plan


def _bias_table_kernel(rpb_ref, o_ref, *, rows):
    base = pl.program_id(0) * (N_DR * N_DC)
    shape = (GRID_W, 2 * GRID_W)
    lane = lax.broadcasted_iota(jnp.int32, shape, 1)
    qc = lax.broadcasted_iota(jnp.int32, shape, 0)
    right = lane >= GRID_W
    kc = jnp.where(right, lane - GRID_W, lane)
    delta = kc - qc + (NA_COLS - 1)
    win_start = jnp.clip(qc - NA_COLS // 2, 0, GRID_W - NA_COLS)
    col_ok = (kc >= win_start) & (kc < win_start + NA_COLS)
    for (dr_left, ok_l, ok_r), dests in _bias_tile_plan(rows).items():
        if dr_left is None:
            tile = jnp.full(shape, MASK_VALUE, F32)
        else:
            acc = jnp.zeros(shape, F32)
            for dc in range(N_DC):
                s_l = rpb_ref[base + dr_left * N_DC + dc] if ok_l else 0.0
                s_r = rpb_ref[base + (dr_left + 1) * N_DC + dc] if ok_r else 0.0
                acc = jnp.where(delta == dc, jnp.where(right, s_r, s_l), acc)
            ok = col_ok if (ok_l and ok_r) else (col_ok & right if ok_r else col_ok & ~right)
            tile = jnp.where(ok, acc, MASK_VALUE)
        for v, g, p in dests:
            o_ref[0, v, g * GRID_W:(g + 1) * GRID_W, p * 2 * GRID_W:(p + 1) * 2 * GRID_W] = tile


def _attn_bias_table(rpb, rows):
    n_heads = rpb.shape[0]
    assert rpb.shape[1:] == (N_DR, N_DC) and rows % Q_ROWS == 0 and rows >= KEY_ROWS + Q_ROWS
    blk = (1, 3, Q_ROWS * GRID_W, KEY_ROWS * GRID_W)
    return pl.pallas_call(
        functools.partial(_bias_table_kernel, rows=rows),
        grid=(n_heads,),
        in_specs=[pl.BlockSpec(memory_space=pltpu.SMEM)],
        out_specs=pl.BlockSpec(blk, lambda h: (h, 0, 0, 0)),
        out_shape=jax.ShapeDtypeStruct((n_heads,) + blk[1:], F32),
        compiler_params=_params(("parallel",)),
        name="bias_table",
    )(rpb.reshape(-1))


def _attn_kernel(q_ref, k_ref, v_ref, tbl_ref, o_ref, *, rows, heads):
    rb = pl.program_id(2)
    r0 = rb * Q_ROWS
    ks = jnp.clip(r0 - NA_ROWS // 2, 0, rows - KEY_ROWS)
    nk = KEY_ROWS * GRID_W
    keys = pl.ds(pl.multiple_of(ks * GRID_W, GRID_W), nk)
    for hd in range(heads):
        cols = slice(hd * HEAD_DIM, (hd + 1) * HEAD_DIM)
        q = q_ref[:, cols]
        k = k_ref[keys, cols]
        v = v_ref[keys, cols]
        s = lax.dot_general(q, k, (((1,), (1,)), ((), ())), preferred_element_type=F32)
        s = s + tbl_ref[hd, 0]
        mx = jnp.max(s, axis=-1, keepdims=True)
        p = jnp.exp(s - mx)
        p = p / jnp.sum(p, axis=-1, keepdims=True)
        o = jnp.dot(p.astype(v.dtype), v, preferred_element_type=F32)
        o_ref[:, cols] = o.astype(o_ref.dtype)


def _attention(qk, v, tbl, batch, seq):
    m, c = v.shape
    rows = seq // GRID_W
    nrb = rows // Q_ROWS
    heads = min(4, c // HEAD_DIM)
    bc = heads * HEAD_DIM
    ngrp = c // bc
    bq = Q_ROWS * GRID_W

    def variant(rb):
        return jnp.where(rb == 0, 0, jnp.where(rb == nrb - 1, 2, 1))

    kern = functools.partial(_attn_kernel, rows=rows, heads=heads)
    return pl.pallas_call(
        kern,
        grid=(batch, ngrp, nrb),
        in_specs=[pl.BlockSpec((bq, bc), lambda b, g, r: (b * nrb + r, g)),
                  pl.BlockSpec((seq, bc), lambda b, g, r: (b, ngrp + g)),
                  pl.BlockSpec((seq, bc), lambda b, g, r: (b, g)),
                  pl.BlockSpec((heads, 1, bq, KEY_ROWS * GRID_W), lambda b, g, r: (g, variant(r), 0, 0))],
        out_specs=pl.BlockSpec((bq, bc), lambda b, g, r: (b * nrb + r, g)),
        out_shape=jax.ShapeDtypeStruct((m, c), BF16),
        compiler_params=_params(("parallel", "parallel", "arbitrary")),
        name="natten",
    )(qk, qk, v, tbl)


def _out_proj_kernel(x_ref, c_ref, a_ref, wc_ref, wa_ref, o_ref):
    acc = jnp.dot(c_ref[...], wc_ref[...], preferred_element_type=F32)
    acc = acc + jnp.dot(a_ref[...], wa_ref[...], preferred_element_type=F32)
    o_ref[...] = x_ref[...] + acc


def _out_proj(x, conv_out, attn_out, w_out):
    m, d = x.shape
    cc = conv_out.shape[1]
    ca = attn_out.shape[1]
    assert cc == ca, "the two head groups share one weight array split at its row midpoint"
    bm = _block(m, 1024)
    bn = _block(d, 512)
    return pl.pallas_call(
        _out_proj_kernel,
        grid=(m // bm, d // bn),
        in_specs=[pl.BlockSpec((bm, bn), lambda i, j: (i, j)),
                  pl.BlockSpec((bm, cc), lambda i, j: (i, 0)),
                  pl.BlockSpec((bm, ca), lambda i, j: (i, 0)),
                  pl.BlockSpec((cc, bn), lambda i, j: (0, j)),
                  pl.BlockSpec((ca, bn), lambda i, j: (1, j))],
        out_specs=pl.BlockSpec((bm, bn), lambda i, j: (i, j)),
        out_shape=jax.ShapeDtypeStruct((m, d), F32),
        compiler_params=_params(("parallel", "arbitrary")),
        name="out_proj",
    )(x, conv_out, attn_out, w_out, w_out)


def kernel(x, g_ffn1, w1_gate, w1_up, w1_down, g_mix, w_in, conv_w, conv_b, conv_ln_g, conv_ln_b, q_norm_g, k_norm_g, rpb, w_out, g_ffn2, w2_gate, w2_up, w2_down, g_final):
    batch, seq, d = x.shape
    depth = g_ffn1.shape[0]
    c_conv = conv_w.shape[2]
    c_attn = w_out.shape[1] - c_conv
    n_heads = c_attn // HEAD_DIM
    rows = seq // GRID_W
    scale = HEAD_DIM ** -0.5

    xs = x.reshape(batch * seq, d)
    for l in range(depth):
        wg1, wu1, wd1 = w1_gate[l].astype(BF16), w1_up[l].astype(BF16), w1_down[l].astype(BF16)
        wg2, wu2, wd2 = w2_gate[l].astype(BF16), w2_up[l].astype(BF16), w2_down[l].astype(BF16)
        w_in_b = w_in[l].astype(BF16)
        w_out_b = w_out[l].astype(BF16)
        qk_gains = jnp.concatenate([jnp.tile(q_norm_g[l] * scale, n_heads), jnp.tile(k_norm_g[l], n_heads)])
        tbl = _attn_bias_table(rpb[l], rows)

        h1 = _rmsnorm(xs, g_ffn1[l], BF16)
        xs = _ffn(xs, h1, wg1, wu1, wd1)

        h2 = _rmsnorm(xs, g_mix[l], BF16)
        u = _proj_glu(h2, w_in_b, c_conv)
        qk = _proj_qk(h2, w_in_b, qk_gains, 2 * c_conv)
        v = _proj_v(h2, w_in_b, 2 * c_conv + 2 * c_attn, c_attn)
        conv_out = _conv_module(u, seq, conv_w[l], conv_b[l], conv_ln_g[l], conv_ln_b[l])
        attn_out = _attention(qk, v, tbl, batch, seq)
        xs = _out_proj(xs, conv_out, attn_out, w_out_b)

        h3 = _rmsnorm(xs, g_ffn2[l], BF16)
        xs = _ffn(xs, h3, wg2, wu2, wd2, g_final=g_final[l])
    return xs.reshape(batch, seq, d)
```

```python
import functools

import jax
import jax.numpy as jnp
from jax import lax
from jax.experimental import pallas as pl
from jax.experimental.pallas import tpu as pltpu

F32 = jnp.float32
BF16 = jnp.bfloat16

EPS = 1e-6
HEAD_DIM = 128
CONV_K = 31
GRID_W = 64
NA_ROWS = 8
NA_COLS = 16
Q_ROWS = 4
KEY_ROWS = Q_ROWS + NA_ROWS
MASK_VALUE = -1e30
SOFTMAX_ROWS = 32

V7X_VMEM_LIMIT = 56 * 1024 * 1024
FFN_VMEM_LIMIT = 60 * 1024 * 1024


def _block(dim, pref):
    b = min(dim, pref)
    while dim % b:
        b //= 2
    return b


def _params(semantics, vmem_limit=V7X_VMEM_LIMIT):
    return pltpu.CompilerParams(dimension_semantics=semantics, vmem_limit_bytes=vmem_limit)


def _rmsnorm_kernel(x_ref, g_ref, o_ref):
    x = x_ref[...]
    ms = jnp.mean(x * x, axis=-1, keepdims=True)
    o_ref[...] = ((x * lax.rsqrt(ms + EPS)) * g_ref[...]).astype(o_ref.dtype)


def _rmsnorm(x, g, out_dtype):
    m, d = x.shape
    bm = _block(m, 256)
    return pl.pallas_call(
        _rmsnorm_kernel,
        grid=(m // bm,),
        in_specs=[pl.BlockSpec((bm, d), lambda i: (i, 0)),
                  pl.BlockSpec((1, d), lambda i: (0, 0))],
        out_specs=pl.BlockSpec((bm, d), lambda i: (i, 0)),
        out_shape=jax.ShapeDtypeStruct((m, d), out_dtype),
        compiler_params=_params(("parallel",)),
        name="rmsnorm",
    )(x, g.reshape(1, d))


FFN_NORM_ROWS = 64


def _ffn_kernel(x_hbm, h_ref, wg_ref, wu_ref, wd_ref, gfin_ref, out_hbm, acc_ref, sem_in, sem_out,
                *, bm, bn, nf, final_norm):
    i = pl.program_id(0)
    f = pl.program_id(1)
    rows = pl.ds(pl.multiple_of(i * bm, bm), bm)
    d = acc_ref.shape[1]
    nc = d // bn
    rc = min(bm, FFN_NORM_ROWS)

    def seed_copy(c):
        cols = pl.ds(c * bn, bn)
        return pltpu.make_async_copy(x_hbm.at[rows, cols], acc_ref.at[:, cols], sem_in.at[c])

    def col_writeback(c):
        cols = pl.ds(c * bn, bn)
        return pltpu.make_async_copy(acc_ref.at[:, cols], out_hbm.at[rows, cols], sem_out.at[c])

    def row_writeback(r):
        src = acc_ref.at[pl.ds(pl.multiple_of(r * rc, rc), rc)]
        dst = out_hbm.at[pl.ds(pl.multiple_of(i * bm + r * rc, rc), rc)]
        return pltpu.make_async_copy(src, dst, sem_out.at[r])

    def step(first, last):
        if first:
            for c in range(nc):
                seed_copy(c).start()
        h = h_ref[...]
        gate = jnp.dot(h, wg_ref[...].astype(BF16), preferred_element_type=F32)
        up = jnp.dot(h, wu_ref[...].astype(BF16), preferred_element_type=F32)
        act = ((0.5 * (gate * jax.nn.sigmoid(gate))) * up).astype(BF16)
        for c in range(nc):
            cols = slice(c * bn, (c + 1) * bn)
            if first:
                seed_copy(c).wait()
            acc_ref[:, cols] += jnp.dot(act, wd_ref[:, cols].astype(BF16), preferred_element_type=F32)
            if last and not final_norm:
                col_writeback(c).start()
        if last and final_norm:
            def norm_rows(r, carry):
                rs = pl.ds(pl.multiple_of(r * rc, rc), rc)
                v = acc_ref[rs, :]
                ms = jnp.mean(v * v, axis=-1, keepdims=True)
                acc_ref[rs, :] = (v * lax.rsqrt(ms + EPS)) * gfin_ref[...]
                row_writeback(r).start()
                return carry

            lax.fori_loop(0, bm // rc, norm_rows, 0)

            def wait_rows(r, carry):
                row_writeback(r).wait()
                return carry

            lax.fori_loop(0, bm // rc, wait_rows, 0)
        elif last:
            for c in range(nc):
                col_writeback(c).wait()

    if nf == 1:
        step(True, True)
    else:
        pl.when(f == 0)(lambda: step(True, False))
        pl.when((f > 0) & (f < nf - 1))(lambda: step(False, False))
        pl.when(f == nf - 1)(lambda: step(False, True))


def _ffn(x, h, wg, wu, wd, g_final=None):
    m, d = x.shape
    ff = wg.shape[1]
    bm = _block(m, 1024)
    bf = _block(ff, 256)
    bn = _block(d, 512)
    nf = ff // bf
    final_norm = g_final is not None
    gfin = (g_final if final_norm else jnp.ones((d,), F32)).reshape(1, d)
    n_out_sems = max(d // bn, bm // min(bm, FFN_NORM_ROWS))
    kern = functools.partial(_ffn_kernel, bm=bm, bn=bn, nf=nf, final_norm=final_norm)
    return pl.pallas_call(
        kern,
        grid=(m // bm, ff // bf),
        in_specs=[pl.BlockSpec(memory_space=pl.ANY),
                  pl.BlockSpec((bm, d), lambda i, f: (i, 0)),
                  pl.BlockSpec((d, bf), lambda i, f: (0, f)),
                  pl.BlockSpec((d, bf), lambda i, f: (0, f)),
                  pl.BlockSpec((bf, d), lambda i, f: (f, 0)),
                  pl.BlockSpec((1, d), lambda i, f: (0, 0))],
        out_specs=pl.BlockSpec(memory_space=pl.ANY),
        out_shape=jax.ShapeDtypeStruct((m, d), F32),
        scratch_shapes=[pltpu.VMEM((bm, d), F32), pltpu.SemaphoreType.DMA((d // bn,)),
                        pltpu.SemaphoreType.DMA((n_out_sems,))],
        compiler_params=_params(("arbitrary", "arbitrary"), FFN_VMEM_LIMIT),
        name="ffn_final" if final_norm else "ffn",
    )(x, h, wg, wu, wd, gfin)


def _proj_glu_kernel(h_ref, wa_ref, wg_ref, o_ref):
    h = h_ref[...]
    a = jnp.dot(h, wa_ref[...].astype(BF16), preferred_element_type=F32)
    g = jnp.dot(h, wg_ref[...].astype(BF16), preferred_element_type=F32)
    o_ref[...] = a * jax.nn.sigmoid(g)


def _proj_glu(h, w_in, c_conv):
    m, d = h.shape
    bm = _block(m, 1024)
    bn = _block(c_conv, 256)
    nb = c_conv // bn
    return pl.pallas_call(
        _proj_glu_kernel,
        grid=(m // bm, nb),
        in_specs=[pl.BlockSpec((bm, d), lambda i, j: (i, 0)),
                  pl.BlockSpec((d, bn), lambda i, j: (0, j)),
                  pl.BlockSpec((d, bn), lambda i, j: (0, j + nb))],
        out_specs=pl.BlockSpec((bm, bn), lambda i, j: (i, j)),
        out_shape=jax.ShapeDtypeStruct((m, c_conv), F32),
        compiler_params=_params(("parallel", "arbitrary")),
        name="proj_glu",
    )(h, w_in, w_in)


def _proj_qk_kernel(h_ref, w_ref, g_ref, o_ref):
    t = jnp.dot(h_ref[...], w_ref[...].astype(BF16), preferred_element_type=F32)
    for hd in range(t.shape[1] // HEAD_DIM):
        cols = slice(hd * HEAD_DIM, (hd + 1) * HEAD_DIM)
        th = t[:, cols]
        ms = jnp.mean(th * th, axis=-1, keepdims=True)
        o_ref[:, cols] = ((th * lax.rsqrt(ms + EPS)) * g_ref[:, cols]).astype(o_ref.dtype)


def _proj_qk(h, w_in, gains, col0):
    m, d = h.shape
    n = gains.shape[0]
    bm = _block(m, 1024)
    bn = _block(n, 512)
    off = col0 // bn
    return pl.pallas_call(
        _proj_qk_kernel,
        grid=(m // bm, n // bn),
        in_specs=[pl.BlockSpec((bm, d), lambda i, j: (i, 0)),
                  pl.BlockSpec((d, bn), lambda i, j: (0, j + off)),
                  pl.BlockSpec((1, bn), lambda i, j: (0, j))],
        out_specs=pl.BlockSpec((bm, bn), lambda i, j: (i, j)),
        out_shape=jax.ShapeDtypeStruct((m, n), BF16),
        compiler_params=_params(("parallel", "arbitrary")),
        name="proj_qk",
    )(h, w_in, gains.reshape(1, n))


def _proj_v_kernel(h_ref, w_ref, o_ref):
    o_ref[...] = jnp.dot(h_ref[...], w_ref[...].astype(BF16), preferred_element_type=F32).astype(o_ref.dtype)


def _proj_v(h, w_in, col0, n):
    m, d = h.shape
    bm = _block(m, 1024)
    bn = _block(n, 512)
    off = col0 // bn
    return pl.pallas_call(
        _proj_v_kernel,
        grid=(m // bm, n // bn),
        in_specs=[pl.BlockSpec((bm, d), lambda i, j: (i, 0)),
                  pl.BlockSpec((d, bn), lambda i, j: (0, j + off))],
        out_specs=pl.BlockSpec((bm, bn), lambda i, j: (i, j)),
        out_shape=jax.ShapeDtypeStruct((m, n), BF16),
        compiler_params=_params(("parallel", "arbitrary")),
        name="proj_v",
    )(h, w_in)


CONV_HALO = 16
CONV_LANES = 256


SUBLANES = 8
CONV_FIRST = CONV_HALO - CONV_K // 2
CONV_SHIFT_ROWS = SUBLANES * ((CONV_FIRST + CONV_K - 1) // SUBLANES)


def _conv_kernel(prev_ref, main_ref, next_ref, w_ref, b_ref, lg_ref, lb_ref, o_ref, xpad_ref, shift_ref, y_ref):
    t = pl.program_id(1)
    ts, c = main_ref.shape
    prev = prev_ref[...]
    nxt = next_ref[...]
    xpad_ref[0:CONV_HALO, :] = jnp.where(t == 0, jnp.zeros_like(prev), prev)
    xpad_ref[CONV_HALO:CONV_HALO + ts, :] = main_ref[...]
    xpad_ref[CONV_HALO + ts:, :] = jnp.where(t == pl.num_programs(1) - 1, jnp.zeros_like(nxt), nxt)
    n_shift = ts + CONV_SHIFT_ROWS

    def chunk(ci, carry):
        lanes = pl.ds(pl.multiple_of(ci * CONV_LANES, CONV_LANES), CONV_LANES)
        for s in range(1, SUBLANES):
            shift_ref[s - 1] = xpad_ref[s:s + n_shift, lanes]
        acc = jnp.zeros((ts, CONV_LANES), F32)
        for k in range(CONV_K):
            a, s = divmod(CONV_FIRST + k, SUBLANES)
            if s == 0:
                xk = xpad_ref[a * SUBLANES:a * SUBLANES + ts, lanes]
            else:
                xk = shift_ref[s - 1, a * SUBLANES:a * SUBLANES + ts, :]
            acc = acc + w_ref[k:k + 1, lanes] * xk
        y_ref[:, lanes] = acc + b_ref[:, lanes]
        return carry

    lax.fori_loop(0, c // CONV_LANES, chunk, 0)

    y = y_ref[...]
    mu = jnp.mean(y, axis=-1, keepdims=True)
    yc = y - mu
    var = jnp.mean(yc * yc, axis=-1, keepdims=True)
    z = (yc * lax.rsqrt(var + EPS)) * lg_ref[...] + lb_ref[...]
    o_ref[...] = (z * jax.nn.sigmoid(z)).astype(o_ref.dtype)


def _conv_module(u, seq, conv_w, conv_b, ln_g, ln_b):
    m, c = u.shape
    ts = _block(seq, 64)
    nt = seq // ts
    hb = ts // CONV_HALO
    nh = m // CONV_HALO
    last_h = seq // CONV_HALO - 1

    def prev_map(b, t):
        return (b * (seq // CONV_HALO) + jnp.maximum(t * hb - 1, 0), 0)

    def next_map(b, t):
        return (b * (seq // CONV_HALO) + jnp.minimum((t + 1) * hb, last_h), 0)

    del nh
    vec = lambda: pl.BlockSpec((1, c), lambda b, t: (0, 0))
    return pl.pallas_call(
        _conv_kernel,
        grid=(m // seq, nt),
        in_specs=[pl.BlockSpec((CONV_HALO, c), prev_map),
                  pl.BlockSpec((ts, c), lambda b, t: (b * nt + t, 0)),
                  pl.BlockSpec((CONV_HALO, c), next_map),
                  pl.BlockSpec((CONV_K, c), lambda b, t: (0, 0)),
                  vec(), vec(), vec()],
        out_specs=pl.BlockSpec((ts, c), lambda b, t: (b * nt + t, 0)),
        out_shape=jax.ShapeDtypeStruct((m, c), BF16),
        scratch_shapes=[pltpu.VMEM((ts + 2 * CONV_HALO, c), F32),
                        pltpu.VMEM((SUBLANES - 1, ts + CONV_SHIFT_ROWS, CONV_LANES), F32),
                        pltpu.VMEM((ts, c), F32)],
        compiler_params=_params(("parallel", "arbitrary")),
        name="conv_module",
    )(u, u, u, conv_w, conv_b.reshape(1, c), ln_g.reshape(1, c), ln_b.reshape(1, c))


N_DR = 2 * NA_ROWS - 1
N_DC = 2 * NA_COLS - 1


def _bias_tile_plan(rows):
    plan = {}
    for v, r0 in enumerate((0, Q_ROWS, rows - Q_ROWS)):
        ks = min(max(r0 - NA_ROWS // 2, 0), rows - KEY_ROWS)
        for g in range(Q_ROWS):
            r = r0 + g
            row_start = min(max(r - NA_ROWS // 2, 0), rows - NA_ROWS)
            for p in range(KEY_ROWS // 2):
                oks = [row_start <= ks + j < row_start + NA_ROWS for j in (2 * p, 2 * p + 1)]
                dr_left = ks + 2 * p - r + NA_ROWS - 1
                key = (dr_left if any(oks) else None, oks[0], oks[1])
                plan.setdefault(key, []).append((v, g, p))
    return plan


def _bias_table_kernel(rpb_ref, o_ref, *, rows):
    base = pl.program_id(0) * (N_DR * N_DC)
    shape = (GRID_W, 2 * GRID_W)
    lane = lax.broadcasted_iota(jnp.int32, shape, 1)
    qc = lax.broadcasted_iota(jnp.int32, shape, 0)
    right = lane >= GRID_W
    kc = jnp.where(right, lane - GRID_W, lane)
    delta = kc - qc + (NA_COLS - 1)
    win_start = jnp.clip(qc - NA_COLS // 2, 0, GRID_W - NA_COLS)
    col_ok = (kc >= win_start) & (kc < win_start + NA_COLS)
    for (dr_left, ok_l, ok_r), dests in _bias_tile_plan(rows).items():
        if dr_left is None:
            tile = jnp.full(shape, MASK_VALUE, F32)
        else:
            acc = jnp.zeros(shape, F32)
            for dc in range(N_DC):
                s_l = rpb_ref[base + dr_left * N_DC + dc] if ok_l else 0.0
                s_r = rpb_ref[base + (dr_left + 1) * N_DC + dc] if ok_r else 0.0
                acc = jnp.where(delta == dc, jnp.where(right, s_r, s_l), acc)
            ok = col_ok if (ok_l and ok_r) else (col_ok & right if ok_r else col_ok & ~right)
            tile = jnp.where(ok, acc, MASK_VALUE)
        for v, g, p in dests:
            o_ref[0, v, g * GRID_W:(g + 1) * GRID_W, p * 2 * GRID_W:(p + 1) * 2 * GRID_W] = tile


def _attn_bias_table(rpb, rows):
    n_heads = rpb.shape[0]
    assert rpb.shape[1:] == (N_DR, N_DC) and rows % Q_ROWS == 0 and rows >= KEY_ROWS + Q_ROWS
    blk = (1, 3, Q_ROWS * GRID_W, KEY_ROWS * GRID_W)
    return pl.pallas_call(
        functools.partial(_bias_table_kernel, rows=rows),
        grid=(n_heads,),
        in_specs=[pl.BlockSpec(memory_space=pltpu.SMEM)],
        out_specs=pl.BlockSpec(blk, lambda h: (h, 0, 0, 0)),
        out_shape=jax.ShapeDtypeStruct((n_heads,) + blk[1:], F32),
        compiler_params=_params(("parallel",)),
        name="bias_table",
    )(rpb.reshape(-1))


def _attn_kernel(q_ref, k_ref, v_ref, tbl_ref, o_ref, s_ref, p_ref, *, rows, heads):
    rb = pl.program_id(2)
    r0 = rb * Q_ROWS
    ks = jnp.clip(r0 - NA_ROWS // 2, 0, rows - KEY_ROWS)
    nk = KEY_ROWS * GRID_W
    keys = pl.ds(pl.multiple_of(ks * GRID_W, GRID_W), nk)
    head_cols = [slice(hd * HEAD_DIM, (hd + 1) * HEAD_DIM) for hd in range(heads)]
    for hd, cols in enumerate(head_cols):
        s = lax.dot_general(q_ref[:, cols], k_ref[keys, cols], (((1,), (1,)), ((), ())),
                            preferred_element_type=F32)
        s_ref[hd] = s + tbl_ref[hd, 0]
    for hd in range(heads):
        s = s_ref[hd]
        p_ref[hd] = jnp.exp(s - jnp.max(s, axis=-1, keepdims=True)).astype(p_ref.dtype)
    ones = jnp.ones((nk, HEAD_DIM), v_ref.dtype)
    for hd, cols in enumerate(head_cols):
        v_ext = jnp.concatenate([v_ref[keys, cols], ones], axis=1)
        o = jnp.dot(p_ref[hd], v_ext, preferred_element_type=F32)
        o_ref[:, cols] = (o[:, :HEAD_DIM] / o[:, HEAD_DIM:]).astype(o_ref.dtype)


def _attention(qk, v, tbl, batch, seq):
    m, c = v.shape
    rows = seq // GRID_W
    nrb = rows // Q_ROWS
    heads = min(4, c // HEAD_DIM)
    bc = heads * HEAD_DIM
    ngrp = c // bc
    bq = Q_ROWS * GRID_W

    def variant(rb):
        return jnp.where(rb == 0, 0, jnp.where(rb == nrb - 1, 2, 1))

    kern = functools.partial(_attn_kernel, rows=rows, heads=heads)
    return pl.pallas_call(
        kern,
        grid=(batch, ngrp, nrb),
        in_specs=[pl.BlockSpec((bq, bc), lambda b, g, r: (b * nrb + r, g)),
                  pl.BlockSpec((seq, bc), lambda b, g, r: (b, ngrp + g)),
                  pl.BlockSpec((seq, bc), lambda b, g, r: (b, g)),
                  pl.BlockSpec((heads, 1, bq, KEY_ROWS * GRID_W), lambda b, g, r: (g, variant(r), 0, 0))],
        out_specs=pl.BlockSpec((bq, bc), lambda b, g, r: (b * nrb + r, g)),
        out_shape=jax.ShapeDtypeStruct((m, c), BF16),
        scratch_shapes=[pltpu.VMEM((heads, bq, KEY_ROWS * GRID_W), F32),
                        pltpu.VMEM((heads, bq, KEY_ROWS * GRID_W), BF16)],
        compiler_params=_params(("parallel", "parallel", "arbitrary")),
        name="natten",
    )(qk, qk, v, tbl)


def _out_proj_kernel(x_ref, c_ref, a_ref, wc_ref, wa_ref, o_ref):
    acc = jnp.dot(c_ref[...], wc_ref[...].astype(BF16), preferred_element_type=F32)
    acc = acc + jnp.dot(a_ref[...], wa_ref[...].astype(BF16), preferred_element_type=F32)
    o_ref[...] = x_ref[...] + acc


def _out_proj(x, conv_out, attn_out, w_out):
    m, d = x.shape
    cc = conv_out.shape[1]
    ca = attn_out.shape[1]
    assert cc == ca, "the two head groups share one weight array split at its row midpoint"
    bm = _block(m, 1024)
    bn = _block(d, 512)
    return pl.pallas_call(
        _out_proj_kernel,
        grid=(m // bm, d // bn),
        in_specs=[pl.BlockSpec((bm, bn), lambda i, j: (i, j)),
                  pl.BlockSpec((bm, cc), lambda i, j: (i, 0)),
                  pl.BlockSpec((bm, ca), lambda i, j: (i, 0)),
                  pl.BlockSpec((cc, bn), lambda i, j: (0, j)),
                  pl.BlockSpec((ca, bn), lambda i, j: (1, j))],
        out_specs=pl.BlockSpec((bm, bn), lambda i, j: (i, j)),
        out_shape=jax.ShapeDtypeStruct((m, d), F32),
        compiler_params=_params(("parallel", "arbitrary")),
        name="out_proj",
    )(x, conv_out, attn_out, w_out, w_out)


def kernel(x, g_ffn1, w1_gate, w1_up, w1_down, g_mix, w_in, conv_w, conv_b, conv_ln_g, conv_ln_b, q_norm_g, k_norm_g, rpb, w_out, g_ffn2, w2_gate, w2_up, w2_down, g_final):
    batch, seq, d = x.shape
    depth = g_ffn1.shape[0]
    c_conv = conv_w.shape[2]
    c_attn = w_out.shape[1] - c_conv
    n_heads = c_attn // HEAD_DIM
    rows = seq // GRID_W
    scale = HEAD_DIM ** -0.5

    xs = x.reshape(batch * seq, d)
    for l in range(depth):
        qk_gains = jnp.concatenate([jnp.tile(q_norm_g[l] * scale, n_heads), jnp.tile(k_norm_g[l], n_heads)])
        tbl = _attn_bias_table(rpb[l], rows)

        h1 = _rmsnorm(xs, g_ffn1[l], BF16)
        xs = _ffn(xs, h1, w1_gate[l], w1_up[l], w1_down[l])

        h2 = _rmsnorm(xs, g_mix[l], BF16)
        u = _proj_glu(h2, w_in[l], c_conv)
        qk = _proj_qk(h2, w_in[l], qk_gains, 2 * c_conv)
        v = _proj_v(h2, w_in[l], 2 * c_conv + 2 * c_attn, c_attn)
        conv_out = _conv_module(u, seq, conv_w[l], conv_b[l], conv_ln_g[l], conv_ln_b[l])
        attn_out = _attention(qk, v, tbl, batch, seq)
        xs = _out_proj(xs, conv_out, attn_out, w_out[l])

        h3 = _rmsnorm(xs, g_ffn2[l], BF16)
        xs = _ffn(xs, h3, w2_gate[l], w2_up[l], w2_down[l], g_final=g_final[l])
    return xs.reshape(batch, seq, d)
```

```python
import functools

import jax
import jax.numpy as jnp
from jax import lax
from jax.experimental import pallas as pl
from jax.experimental.pallas import tpu as pltpu

F32 = jnp.float32
BF16 = jnp.bfloat16

EPS = 1e-6
HEAD_DIM = 128
CONV_K = 31
GRID_W = 64
NA_ROWS = 8
NA_COLS = 16
Q_ROWS = 4
KEY_ROWS = Q_ROWS + NA_ROWS
MASK_VALUE = -1e30
SOFTMAX_ROWS = 32

V7X_VMEM_LIMIT = 56 * 1024 * 1024
FFN_VMEM_LIMIT = 60 * 1024 * 1024


def _block(dim, pref):
    b = min(dim, pref)
    while dim % b:
        b //= 2
    return b


def _params(semantics, vmem_limit=V7X_VMEM_LIMIT):
    return pltpu.CompilerParams(dimension_semantics=semantics, vmem_limit_bytes=vmem_limit)


FFN_NORM_ROWS = 64


def _rms_rows(v, g):
    ms = jnp.mean(v * v, axis=-1, keepdims=True)
    return (v * lax.rsqrt(ms + EPS)) * g


def _ffn_kernel(*refs, bm, bn, nf, nblk, n_look, final_norm):
    if final_norm:
        (x_hbm, gin_ref, wg_ref, wu_ref, wd_ref, gout_ref, out_hbm,
         acc_ref, h_scr, xst_ref, sem_in, sem_out, sem_look) = refs
        hnext_hbm = hst_ref = sem_h = None
    else:
        (x_hbm, gin_ref, wg_ref, wu_ref, wd_ref, gout_ref, out_hbm, hnext_hbm,
         acc_ref, h_scr, xst_ref, hst_ref, sem_in, sem_out, sem_look, sem_h) = refs
    i = pl.program_id(0)
    f = pl.program_id(1)
    t = i * nf + f
    rows = pl.ds(pl.multiple_of(i * bm, bm), bm)
    d = acc_ref.shape[1]
    nc = d // bn
    rc = min(bm, FFN_NORM_ROWS)
    lr = bm // n_look

    def look_copy(blk, chunk, slot):
        src = x_hbm.at[pl.ds(pl.multiple_of(blk * bm + chunk * lr, lr), lr)]
        return pltpu.make_async_copy(src, xst_ref.at[slot], sem_look.at[slot])

    def look_target(blk_i, step_f):
        return jnp.minimum(blk_i + 1, nblk - 1), jnp.minimum(step_f, n_look - 1)

    def normalise_chunk(slot, h_slot, chunk):
        hn = _rms_rows(xst_ref[slot], gin_ref[...]).astype(BF16)
        h_scr[h_slot, pl.ds(pl.multiple_of(chunk * lr, lr), lr), :] = hn

    def hnext_copy(r, slot):
        dst = hnext_hbm.at[pl.ds(pl.multiple_of(i * bm + r * rc, rc), rc)]
        return pltpu.make_async_copy(hst_ref.at[slot], dst, sem_h.at[slot])

    def seed_copy(c):
        cols = pl.ds(c * bn, bn)
        return pltpu.make_async_copy(x_hbm.at[rows, cols], acc_ref.at[:, cols], sem_in.at[c])

    def col_writeback(c):
        cols = pl.ds(c * bn, bn)
        return pltpu.make_async_copy(acc_ref.at[:, cols], out_hbm.at[rows, cols], sem_out.at[c])

    def row_writeback(r):
        src = acc_ref.at[pl.ds(pl.multiple_of(r * rc, rc), rc)]
        dst = out_hbm.at[pl.ds(pl.multiple_of(i * bm + r * rc, rc), rc)]
        return pltpu.make_async_copy(src, dst, sem_out.at[r])

    def prologue():
        look_copy(0, 0, 0).start()

        def first_block_chunk(c, carry):
            slot = c % 2

            @pl.when(c + 1 < n_look)
            def _():
                look_copy(0, c + 1, 1 - slot).start()

            look_copy(0, c, slot).wait()
            normalise_chunk(slot, 0, c)
            return carry

        lax.fori_loop(0, n_look, first_block_chunk, 0)
        for t0 in range(min(2, nblk * nf)):
            look_copy(min(t0 // nf + 1, nblk - 1), min(t0 % nf, n_look - 1), t0 % 2).start()

    def step(first, last):
        if first:
            pl.when(i == 0)(prologue)
            for c in range(nc):
                seed_copy(c).start()
        slot = t % 2
        look_blk, look_chunk = look_target(i, f)
        look_copy(look_blk, look_chunk, slot).wait()
        h = h_scr[i % 2]
        gate = jnp.dot(h, wg_ref[...].astype(BF16), preferred_element_type=F32)
        up = jnp.dot(h, wu_ref[...].astype(BF16), preferred_element_type=F32)
        act = ((0.5 * (gate * jax.nn.sigmoid(gate))) * up).astype(BF16)
        for c in range(nc):
            cols = slice(c * bn, (c + 1) * bn)
            if first:
                seed_copy(c).wait()
            acc_ref[:, cols] += jnp.dot(act, wd_ref[:, cols].astype(BF16), preferred_element_type=F32)
            if last and not final_norm:
                col_writeback(c).start()
        normalise_chunk(slot, (i + 1) % 2, look_chunk)
        n_rows = bm // rc
        if last and final_norm:
            def norm_rows(r, carry):
                rs = pl.ds(pl.multiple_of(r * rc, rc), rc)
                acc_ref[rs, :] = _rms_rows(acc_ref[rs, :], gout_ref[...])
                row_writeback(r).start()
                return carry

            lax.fori_loop(0, n_rows, norm_rows, 0)

            def wait_rows(r, carry):
                row_writeback(r).wait()
                return carry

            lax.fori_loop(0, n_rows, wait_rows, 0)
        elif last:
            def emit_rows(r, carry):
                h_slot = r % 2

                @pl.when(r >= 2)
                def _():
                    hnext_copy(r - 2, h_slot).wait()

                rs = pl.ds(pl.multiple_of(r * rc, rc), rc)
                hst_ref[h_slot] = _rms_rows(acc_ref[rs, :], gout_ref[...]).astype(BF16)
                hnext_copy(r, h_slot).start()
                return carry

            lax.fori_loop(0, n_rows, emit_rows, 0)
            for r in range(max(n_rows - 2, 0), n_rows):
                hnext_copy(r, r % 2).wait()
            for c in range(nc):
                col_writeback(c).wait()

        t2 = t + 2

        @pl.when(t2 < nblk * nf)
        def _():
            blk2, chunk2 = look_target(lax.div(t2, nf), lax.rem(t2, nf))
            look_copy(blk2, chunk2, slot).start()

    if nf == 1:
        step(True, True)
    else:
        pl.when(f == 0)(lambda: step(True, False))
        pl.when((f > 0) & (f < nf - 1))(lambda: step(False, False))
        pl.when(f == nf - 1)(lambda: step(False, True))


def _ffn(x, g_in, wg, wu, wd, g_out, final_norm):
    m, d = x.shape
    ff = wg.shape[1]
    bm = _block(m, 1024)
    bf = _block(ff, 256)
    bn = _block(d, 512)
    nf = ff // bf
    nblk = m // bm
    n_look = 1
    while 2 * n_look <= nf and bm % (2 * n_look) == 0 and (bm // (2 * n_look)) % 16 == 0:
        n_look *= 2
    rc = min(bm, FFN_NORM_ROWS)
    n_out_sems = max(d // bn, bm // rc)
    kern = functools.partial(_ffn_kernel, bm=bm, bn=bn, nf=nf, nblk=nblk, n_look=n_look, final_norm=final_norm)
    vec = pl.BlockSpec((1, d), lambda i, f: (0, 0))
    any_spec = pl.BlockSpec(memory_space=pl.ANY)
    out_f32 = jax.ShapeDtypeStruct((m, d), F32)
    scratch = [pltpu.VMEM((bm, d), F32),
               pltpu.VMEM((2, bm, d), BF16),
               pltpu.VMEM((2, bm // n_look, d), F32)]
    sems = [pltpu.SemaphoreType.DMA((d // bn,)), pltpu.SemaphoreType.DMA((n_out_sems,)),
            pltpu.SemaphoreType.DMA((2,))]
    if final_norm:
        out_specs, out_shape = any_spec, out_f32
    else:
        out_specs = (any_spec, any_spec)
        out_shape = (out_f32, jax.ShapeDtypeStruct((m, d), BF16))
        scratch.append(pltpu.VMEM((2, rc, d), BF16))
        sems.append(pltpu.SemaphoreType.DMA((2,)))
    return pl.pallas_call(
        kern,
        grid=(nblk, nf),
        in_specs=[any_spec, vec,
                  pl.BlockSpec((d, bf), lambda i, f: (0, f)),
                  pl.BlockSpec((d, bf), lambda i, f: (0, f)),
                  pl.BlockSpec((bf, d), lambda i, f: (f, 0)),
                  vec],
        out_specs=out_specs,
        out_shape=out_shape,
        scratch_shapes=scratch + sems,
        compiler_params=_params(("arbitrary", "arbitrary"), FFN_VMEM_LIMIT),
        name="ffn_final" if final_norm else "ffn",
    )(x, g_in.reshape(1, d), wg, wu, wd, g_out.reshape(1, d))


def _proj_glu_kernel(h_ref, wa_ref, wg_ref, o_ref):
    h = h_ref[...]
    a = jnp.dot(h, wa_ref[...].astype(BF16), preferred_element_type=F32)
    g = jnp.dot(h, wg_ref[...].astype(BF16), preferred_element_type=F32)
    o_ref[...] = a * jax.nn.sigmoid(g)


def _proj_glu(h, w_in, c_conv):
    m, d = h.shape
    bm = _block(m, 1024)
    bn = _block(c_conv, 256)
    nb = c_conv // bn
    return pl.pallas_call(
        _proj_glu_kernel,
        grid=(m // bm, nb),
        in_specs=[pl.BlockSpec((bm, d), lambda i, j: (i, 0)),
                  pl.BlockSpec((d, bn), lambda i, j: (0, j)),
                  pl.BlockSpec((d, bn), lambda i, j: (0, j + nb))],
        out_specs=pl.BlockSpec((bm, bn), lambda i, j: (i, j)),
        out_shape=jax.ShapeDtypeStruct((m, c_conv), F32),
        compiler_params=_params(("parallel", "arbitrary")),
        name="proj_glu",
    )(h, w_in, w_in)


PROJ_ROW_SPLIT = 4


def _row_parts(bm):
    rs = bm // PROJ_ROW_SPLIT if bm % (16 * PROJ_ROW_SPLIT) == 0 else bm
    return [slice(r0, r0 + rs) for r0 in range(0, bm, rs)]


def _proj_qk_kernel(h_ref, w_ref, g_ref, o_ref):
    w = w_ref[...].astype(BF16)
    for rows in _row_parts(h_ref.shape[0]):
        t = jnp.dot(h_ref[rows, :], w, preferred_element_type=F32)
        for hd in range(t.shape[1] // HEAD_DIM):
            cols = slice(hd * HEAD_DIM, (hd + 1) * HEAD_DIM)
            th = t[:, cols]
            ms = jnp.mean(th * th, axis=-1, keepdims=True)
            o_ref[rows, cols] = ((th * lax.rsqrt(ms + EPS)) * g_ref[:, cols]).astype(o_ref.dtype)


def _proj_qk(h, w_in, gains, col0):
    m, d = h.shape
    n = gains.shape[0]
    bm = _block(m, 1024)
    bn = _block(n, 512)
    off = col0 // bn
    return pl.pallas_call(
        _proj_qk_kernel,
        grid=(m // bm, n // bn),
        in_specs=[pl.BlockSpec((bm, d), lambda i, j: (i, 0)),
                  pl.BlockSpec((d, bn), lambda i, j: (0, j + off)),
                  pl.BlockSpec((1, bn), lambda i, j: (0, j))],
        out_specs=pl.BlockSpec((bm, bn), lambda i, j: (i, j)),
        out_shape=jax.ShapeDtypeStruct((m, n), BF16),
        compiler_params=_params(("parallel", "arbitrary")),
        name="proj_qk",
    )(h, w_in, gains.reshape(1, n))


def _proj_v_kernel(h_ref, w_ref, o_ref):
    o_ref[...] = jnp.dot(h_ref[...], w_ref[...].astype(BF16), preferred_element_type=F32).astype(o_ref.dtype)


def _proj_v(h, w_in, col0, n):
    m, d = h.shape
    bm = _block(m, 1024)
    bn = _block(n, 512)
    off = col0 // bn
    return pl.pallas_call(
        _proj_v_kernel,
        grid=(m // bm, n // bn),
        in_specs=[pl.BlockSpec((bm, d), lambda i, j: (i, 0)),
                  pl.BlockSpec((d, bn), lambda i, j: (0, j + off))],
        out_specs=pl.BlockSpec((bm, bn), lambda i, j: (i, j)),
        out_shape=jax.ShapeDtypeStruct((m, n), BF16),
        compiler_params=_params(("parallel", "arbitrary")),
        name="proj_v",
    )(h, w_in)


CONV_HALO = 16
CONV_LANES = 256


SUBLANES = 8
CONV_FIRST = CONV_HALO - CONV_K // 2
CONV_SHIFT_ROWS = SUBLANES * ((CONV_FIRST + CONV_K - 1) // SUBLANES)


def _conv_kernel(prev_ref, main_ref, next_ref, w_ref, b_ref, lg_ref, lb_ref, o_ref, xpad_ref, shift_ref, y_ref):
    t = pl.program_id(1)
    ts, c = main_ref.shape
    prev = prev_ref[...]
    nxt = next_ref[...]
    xpad_ref[0:CONV_HALO, :] = jnp.where(t == 0, jnp.zeros_like(prev), prev)
    xpad_ref[CONV_HALO:CONV_HALO + ts, :] = main_ref[...]
    xpad_ref[CONV_HALO + ts:, :] = jnp.where(t == pl.num_programs(1) - 1, jnp.zeros_like(nxt), nxt)
    n_shift = ts + CONV_SHIFT_ROWS

    def chunk(ci, carry):
        lanes = pl.ds(pl.multiple_of(ci * CONV_LANES, CONV_LANES), CONV_LANES)
        for s in range(1, SUBLANES):
            shift_ref[s - 1] = xpad_ref[s:s + n_shift, lanes]
        acc = jnp.zeros((ts, CONV_LANES), F32)
        for k in range(CONV_K):
            a, s = divmod(CONV_FIRST + k, SUBLANES)
            if s == 0:
                xk = xpad_ref[a * SUBLANES:a * SUBLANES + ts, lanes]
            else:
                xk = shift_ref[s - 1, a * SUBLANES:a * SUBLANES + ts, :]
            acc = acc + w_ref[k:k + 1, lanes] * xk
        y_ref[:, lanes] = acc + b_ref[:, lanes]
        return carry

    lax.fori_loop(0, c // CONV_LANES, chunk, 0)

    y = y_ref[...]
    mu = jnp.mean(y, axis=-1, keepdims=True)
    yc = y - mu
    var = jnp.mean(yc * yc, axis=-1, keepdims=True)
    z = (yc * lax.rsqrt(var + EPS)) * lg_ref[...] + lb_ref[...]
    o_ref[...] = (z * jax.nn.sigmoid(z)).astype(o_ref.dtype)


def _conv_module(u, seq, conv_w, conv_b, ln_g, ln_b):
    m, c = u.shape
    ts = _block(seq, 64)
    nt = seq // ts
    hb = ts // CONV_HALO
    nh = m // CONV_HALO
    last_h = seq // CONV_HALO - 1

    def prev_map(b, t):
        return (b * (seq // CONV_HALO) + jnp.maximum(t * hb - 1, 0), 0)

    def next_map(b, t):
        return (b * (seq // CONV_HALO) + jnp.minimum((t + 1) * hb, last_h), 0)

    del nh
    vec = lambda: pl.BlockSpec((1, c), lambda b, t: (0, 0))
    return pl.pallas_call(
        _conv_kernel,
        grid=(m // seq, nt),
        in_specs=[pl.BlockSpec((CONV_HALO, c), prev_map),
                  pl.BlockSpec((ts, c), lambda b, t: (b * nt + t, 0)),
                  pl.BlockSpec((CONV_HALO, c), next_map),
                  pl.BlockSpec((CONV_K, c), lambda b, t: (0, 0)),
                  vec(), vec(), vec()],
        out_specs=pl.BlockSpec((ts, c), lambda b, t: (b * nt + t, 0)),
        out_shape=jax.ShapeDtypeStruct((m, c), BF16),
        scratch_shapes=[pltpu.VMEM((ts + 2 * CONV_HALO, c), F32),
                        pltpu.VMEM((SUBLANES - 1, ts + CONV_SHIFT_ROWS, CONV_LANES), F32),
                        pltpu.VMEM((ts, c), F32)],
        compiler_params=_params(("parallel", "arbitrary")),
        name="conv_module",
    )(u, u, u, conv_w, conv_b.reshape(1, c), ln_g.reshape(1, c), ln_b.reshape(1, c))


N_DR = 2 * NA_ROWS - 1
N_DC = 2 * NA_COLS - 1


def _bias_tile_plan(rows):
    plan = {}
    for v, r0 in enumerate((0, Q_ROWS, rows - Q_ROWS)):
        ks = min(max(r0 - NA_ROWS // 2, 0), rows - KEY_ROWS)
        for g in range(Q_ROWS):
            r = r0 + g
            row_start = min(max(r - NA_ROWS // 2, 0), rows - NA_ROWS)
            for p in range(KEY_ROWS // 2):
                oks = [row_start <= ks + j < row_start + NA_ROWS for j in (2 * p, 2 * p + 1)]
                dr_left = ks + 2 * p - r + NA_ROWS - 1
                key = (dr_left if any(oks) else None, oks[0], oks[1])
                plan.setdefault(key, []).append((v, g, p))
    return plan


def _bias_table_kernel(rpb_ref, o_ref, *, rows):
    base = pl.program_id(0) * (N_DR * N_DC)
    shape = (GRID_W, 2 * GRID_W)
    lane = lax.broadcasted_iota(jnp.int32, shape, 1)
    qc = lax.broadcasted_iota(jnp.int32, shape, 0)
    right = lane >= GRID_W
    kc = jnp.where(right, lane - GRID_W, lane)
    delta = kc - qc + (NA_COLS - 1)
    win_start = jnp.clip(qc - NA_COLS // 2, 0, GRID_W - NA_COLS)
    col_ok = (kc >= win_start) & (kc < win_start + NA_COLS)
    for (dr_left, ok_l, ok_r), dests in _bias_tile_plan(rows).items():
        if dr_left is None:
            tile = jnp.full(shape, MASK_VALUE, F32)
        else:
            acc = jnp.zeros(shape, F32)
            for dc in range(N_DC):
                s_l = rpb_ref[base + dr_left * N_DC + dc] if ok_l else 0.0
                s_r = rpb_ref[base + (dr_left + 1) * N_DC + dc] if ok_r else 0.0
                acc = jnp.where(delta == dc, jnp.where(right, s_r, s_l), acc)
            ok = col_ok if (ok_l and ok_r) else (col_ok & right if ok_r else col_ok & ~right)
            tile = jnp.where(ok, acc, MASK_VALUE)
        for v, g, p in dests:
            o_ref[0, v, g * GRID_W:(g + 1) * GRID_W, p * 2 * GRID_W:(p + 1) * 2 * GRID_W] = tile


def _attn_bias_table(rpb, rows):
    n_heads = rpb.shape[0]
    assert rpb.shape[1:] == (N_DR, N_DC) and rows % Q_ROWS == 0 and rows >= KEY_ROWS + Q_ROWS
    blk = (1, 3, Q_ROWS * GRID_W, KEY_ROWS * GRID_W)
    return pl.pallas_call(
        functools.partial(_bias_table_kernel, rows=rows),
        grid=(n_heads,),
        in_specs=[pl.BlockSpec(memory_space=pltpu.SMEM)],
        out_specs=pl.BlockSpec(blk, lambda h: (h, 0, 0, 0)),
        out_shape=jax.ShapeDtypeStruct((n_heads,) + blk[1:], F32),
        compiler_params=_params(("parallel",)),
        name="bias_table",
    )(rpb.reshape(-1))


def _attn_kernel(q_ref, k_ref, v_ref, tbl_ref, o_ref, s_ref, p_ref, *, rows, heads):
    rb = pl.program_id(2)
    r0 = rb * Q_ROWS
    ks = jnp.clip(r0 - NA_ROWS // 2, 0, rows - KEY_ROWS)
    nk = KEY_ROWS * GRID_W
    keys = pl.ds(pl.multiple_of(ks * GRID_W, GRID_W), nk)
    head_cols = [slice(hd * HEAD_DIM, (hd + 1) * HEAD_DIM) for hd in range(heads)]
    for hd, cols in enumerate(head_cols):
        s = lax.dot_general(q_ref[:, cols], k_ref[keys, cols], (((1,), (1,)), ((), ())),
                            preferred_element_type=F32)
        s_ref[hd] = s + tbl_ref[hd, 0]
    for hd in range(heads):
        s = s_ref[hd]
        p_ref[hd] = jnp.exp(s - jnp.max(s, axis=-1, keepdims=True)).astype(p_ref.dtype)
    ones = jnp.ones((nk, HEAD_DIM), v_ref.dtype)
    for hd, cols in enumerate(head_cols):
        v_ext = jnp.concatenate([v_ref[keys, cols], ones], axis=1)
        o = jnp.dot(p_ref[hd], v_ext, preferred_element_type=F32)
        o_ref[:, cols] = (o[:, :HEAD_DIM] / o[:, HEAD_DIM:]).astype(o_ref.dtype)


def _attention(qk, v, tbl, batch, seq):
    m, c = v.shape
    rows = seq // GRID_W
    nrb = rows // Q_ROWS
    heads = min(4, c // HEAD_DIM)
    bc = heads * HEAD_DIM
    ngrp = c // bc
    bq = Q_ROWS * GRID_W

    def variant(rb):
        return jnp.where(rb == 0, 0, jnp.where(rb == nrb - 1, 2, 1))

    kern = functools.partial(_attn_kernel, rows=rows, heads=heads)
    return pl.pallas_call(
        kern,
        grid=(batch, ngrp, nrb),
        in_specs=[pl.BlockSpec((bq, bc), lambda b, g, r: (b * nrb + r, g)),
                  pl.BlockSpec((seq, bc), lambda b, g, r: (b, ngrp + g)),
                  pl.BlockSpec((seq, bc), lambda b, g, r: (b, g)),
                  pl.BlockSpec((heads, 1, bq, KEY_ROWS * GRID_W), lambda b, g, r: (g, variant(r), 0, 0))],
        out_specs=pl.BlockSpec((bq, bc), lambda b, g, r: (b * nrb + r, g)),
        out_shape=jax.ShapeDtypeStruct((m, c), BF16),
        scratch_shapes=[pltpu.VMEM((heads, bq, KEY_ROWS * GRID_W), F32),
                        pltpu.VMEM((heads, bq, KEY_ROWS * GRID_W), BF16)],
        compiler_params=_params(("parallel", "parallel", "arbitrary")),
        name="natten",
    )(qk, qk, v, tbl)


def _out_proj_kernel(x_ref, c_ref, a_ref, wc_ref, wa_ref, o_ref):
    acc = jnp.dot(c_ref[...], wc_ref[...].astype(BF16), preferred_element_type=F32)
    acc = acc + jnp.dot(a_ref[...], wa_ref[...].astype(BF16), preferred_element_type=F32)
    o_ref[...] = x_ref[...] + acc


def _out_proj(x, conv_out, attn_out, w_out):
    m, d = x.shape
    cc = conv_out.shape[1]
    ca = attn_out.shape[1]
    assert cc == ca, "the two head groups share one weight array split at its row midpoint"
    bm = _block(m, 1024)
    bn = _block(d, 512)
    return pl.pallas_call(
        _out_proj_kernel,
        grid=(m // bm, d // bn),
        in_specs=[pl.BlockSpec((bm, bn), lambda i, j: (i, j)),
                  pl.BlockSpec((bm, cc), lambda i, j: (i, 0)),
                  pl.BlockSpec((bm, ca), lambda i, j: (i, 0)),
                  pl.BlockSpec((cc, bn), lambda i, j: (0, j)),
                  pl.BlockSpec((ca, bn), lambda i, j: (1, j))],
        out_specs=pl.BlockSpec((bm, bn), lambda i, j: (i, j)),
        out_shape=jax.ShapeDtypeStruct((m, d), F32),
        compiler_params=_params(("parallel", "arbitrary")),
        name="out_proj",
    )(x, conv_out, attn_out, w_out, w_out)


def kernel(x, g_ffn1, w1_gate, w1_up, w1_down, g_mix, w_in, conv_w, conv_b, conv_ln_g, conv_ln_b, q_norm_g, k_norm_g, rpb, w_out, g_ffn2, w2_gate, w2_up, w2_down, g_final):
    batch, seq, d = x.shape
    depth = g_ffn1.shape[0]
    c_conv = conv_w.shape[2]
    c_attn = w_out.shape[1] - c_conv
    n_heads = c_attn // HEAD_DIM
    rows = seq // GRID_W
    scale = HEAD_DIM ** -0.5

    xs = x.reshape(batch * seq, d)
    for l in range(depth):
        qk_gains = jnp.concatenate([jnp.tile(q_norm_g[l] * scale, n_heads), jnp.tile(k_norm_g[l], n_heads)])
        tbl = _attn_bias_table(rpb[l], rows)

        xs, h2 = _ffn(xs, g_ffn1[l], w1_gate[l], w1_up[l], w1_down[l], g_mix[l], final_norm=False)
        u = _proj_glu(h2, w_in[l], c_conv)
        qk = _proj_qk(h2, w_in[l], qk_gains, 2 * c_conv)
        v = _proj_v(h2, w_in[l], 2 * c_conv + 2 * c_attn, c_attn)
        conv_out = _conv_module(u, seq, conv_w[l], conv_b[l], conv_ln_g[l], conv_ln_b[l])
        attn_out = _attention(qk, v, tbl, batch, seq)
        xs = _out_proj(xs, conv_out, attn_out, w_out[l])

        xs = _ffn(xs, g_ffn2[l], w2_gate[l], w2_up[l], w2_down[l], g_final[l], final_norm=True)
    return xs.reshape(batch, seq, d)
```

```python
import functools

import jax
import jax.numpy as jnp
from jax import lax
from jax.experimental import pallas as pl
from jax.experimental.pallas import tpu as pltpu

F32 = jnp.float32
BF16 = jnp.bfloat16

EPS = 1e-6
HEAD_DIM = 128
CONV_K = 31
GRID_W = 64
NA_ROWS = 8
NA_COLS = 16
Q_ROWS = 4
KEY_ROWS = Q_ROWS + NA_ROWS
MASK_VALUE = -1e30
SOFTMAX_ROWS = 32

V7X_VMEM_LIMIT = 56 * 1024 * 1024
FFN_VMEM_LIMIT = 60 * 1024 * 1024


def _block(dim, pref):
    b = min(dim, pref)
    while dim % b:
        b //= 2
    return b


def _params(semantics, vmem_limit=V7X_VMEM_LIMIT):
    return pltpu.CompilerParams(dimension_semantics=semantics, vmem_limit_bytes=vmem_limit)


FFN_NORM_ROWS = 64


def _rms_rows(v, g):
    ms = jnp.mean(v * v, axis=-1, keepdims=True)
    return (v * lax.rsqrt(ms + EPS)) * g


def _ffn_kernel(*refs, bm, bn, nf, nblk, n_look, final_norm):
    if final_norm:
        (x_hbm, gin_ref, wg_ref, wu_ref, wd_ref, gout_ref, out_hbm,
         acc_ref, h_scr, xst_ref, sem_in, sem_out, sem_look) = refs
        hnext_hbm = hst_ref = sem_h = None
    else:
        (x_hbm, gin_ref, wg_ref, wu_ref, wd_ref, gout_ref, out_hbm, hnext_hbm,
         acc_ref, h_scr, xst_ref, hst_ref, sem_in, sem_out, sem_look, sem_h) = refs
    i = pl.program_id(0)
    f = pl.program_id(1)
    t = i * nf + f
    rows = pl.ds(pl.multiple_of(i * bm, bm), bm)
    d = acc_ref.shape[1]
    nc = d // bn
    rc = min(bm, FFN_NORM_ROWS)
    lr = bm // n_look

    def look_copy(blk, chunk, slot):
        src = x_hbm.at[pl.ds(pl.multiple_of(blk * bm + chunk * lr, lr), lr)]
        return pltpu.make_async_copy(src, xst_ref.at[slot], sem_look.at[slot])

    def look_target(blk_i, step_f):
        return jnp.minimum(blk_i + 1, nblk - 1), jnp.minimum(step_f, n_look - 1)

    def normalise_chunk(slot, h_slot, chunk):
        hn = _rms_rows(xst_ref[slot], gin_ref[...]).astype(BF16)
        h_scr[h_slot, pl.ds(pl.multiple_of(chunk * lr, lr), lr), :] = hn

    def hnext_copy(r, slot):
        dst = hnext_hbm.at[pl.ds(pl.multiple_of(i * bm + r * rc, rc), rc)]
        return pltpu.make_async_copy(hst_ref.at[slot], dst, sem_h.at[slot])

    def seed_copy(c):
        cols = pl.ds(c * bn, bn)
        return pltpu.make_async_copy(x_hbm.at[rows, cols], acc_ref.at[:, cols], sem_in.at[c])

    def col_writeback(c):
        cols = pl.ds(c * bn, bn)
        return pltpu.make_async_copy(acc_ref.at[:, cols], out_hbm.at[rows, cols], sem_out.at[c])

    def row_writeback(r):
        src = acc_ref.at[pl.ds(pl.multiple_of(r * rc, rc), rc)]
        dst = out_hbm.at[pl.ds(pl.multiple_of(i * bm + r * rc, rc), rc)]
        return pltpu.make_async_copy(src, dst, sem_out.at[r])

    def prologue():
        look_copy(0, 0, 0).start()

        def first_block_chunk(c, carry):
            slot = c % 2

            @pl.when(c + 1 < n_look)
            def _():
                look_copy(0, c + 1, 1 - slot).start()

            look_copy(0, c, slot).wait()
            normalise_chunk(slot, 0, c)
            return carry

        lax.fori_loop(0, n_look, first_block_chunk, 0)
        for t0 in range(min(2, nblk * nf)):
            look_copy(min(t0 // nf + 1, nblk - 1), min(t0 % nf, n_look - 1), t0 % 2).start()

    def step(first, last):
        if first:
            pl.when(i == 0)(prologue)
            for c in range(nc):
                seed_copy(c).start()
        slot = t % 2
        look_blk, look_chunk = look_target(i, f)
        look_copy(look_blk, look_chunk, slot).wait()
        h = h_scr[i % 2]
        gate = jnp.dot(h, wg_ref[...].astype(BF16), preferred_element_type=F32)
        up = jnp.dot(h, wu_ref[...].astype(BF16), preferred_element_type=F32)
        act = ((0.5 * (gate * jax.nn.sigmoid(gate))) * up).astype(BF16)
        for c in range(nc):
            cols = slice(c * bn, (c + 1) * bn)
            if first:
                seed_copy(c).wait()
            acc_ref[:, cols] += jnp.dot(act, wd_ref[:, cols].astype(BF16), preferred_element_type=F32)
            if last and not final_norm:
                col_writeback(c).start()
        normalise_chunk(slot, (i + 1) % 2, look_chunk)
        n_rows = bm // rc
        if last and final_norm:
            def norm_rows(r, carry):
                rs = pl.ds(pl.multiple_of(r * rc, rc), rc)
                acc_ref[rs, :] = _rms_rows(acc_ref[rs, :], gout_ref[...])
                row_writeback(r).start()
                return carry

            lax.fori_loop(0, n_rows, norm_rows, 0)

            def wait_rows(r, carry):
                row_writeback(r).wait()
                return carry

            lax.fori_loop(0, n_rows, wait_rows, 0)
        elif last:
            def emit_rows(r, carry):
                h_slot = r % 2

                @pl.when(r >= 2)
                def _():
                    hnext_copy(r - 2, h_slot).wait()

                rs = pl.ds(pl.multiple_of(r * rc, rc), rc)
                hst_ref[h_slot] = _rms_rows(acc_ref[rs, :], gout_ref[...]).astype(BF16)
                hnext_copy(r, h_slot).start()
                return carry

            lax.fori_loop(0, n_rows, emit_rows, 0)
            for r in range(max(n_rows - 2, 0), n_rows):
                hnext_copy(r, r % 2).wait()
            for c in range(nc):
                col_writeback(c).wait()

        t2 = t + 2

        @pl.when(t2 < nblk * nf)
        def _():
            blk2, chunk2 = look_target(lax.div(t2, nf), lax.rem(t2, nf))
            look_copy(blk2, chunk2, slot).start()

    if nf == 1:
        step(True, True)
    else:
        pl.when(f == 0)(lambda: step(True, False))
        pl.when((f > 0) & (f < nf - 1))(lambda: step(False, False))
        pl.when(f == nf - 1)(lambda: step(False, True))


def _ffn(x, g_in, wg, wu, wd, g_out, final_norm):
    m, d = x.shape
    ff = wg.shape[1]
    bm = _block(m, 1024)
    bf = _block(ff, 256)
    bn = _block(d, 512)
    nf = ff // bf
    nblk = m // bm
    n_look = 1
    while 2 * n_look <= nf and bm % (2 * n_look) == 0 and (bm // (2 * n_look)) % 16 == 0:
        n_look *= 2
    rc = min(bm, FFN_NORM_ROWS)
    n_out_sems = max(d // bn, bm // rc)
    kern = functools.partial(_ffn_kernel, bm=bm, bn=bn, nf=nf, nblk=nblk, n_look=n_look, final_norm=final_norm)
    vec = pl.BlockSpec((1, d), lambda i, f: (0, 0))
    any_spec = pl.BlockSpec(memory_space=pl.ANY)
    out_f32 = jax.ShapeDtypeStruct((m, d), F32)
    scratch = [pltpu.VMEM((bm, d), F32),
               pltpu.VMEM((2, bm, d), BF16),
               pltpu.VMEM((2, bm // n_look, d), F32)]
    sems = [pltpu.SemaphoreType.DMA((d // bn,)), pltpu.SemaphoreType.DMA((n_out_sems,)),
            pltpu.SemaphoreType.DMA((2,))]
    if final_norm:
        out_specs, out_shape = any_spec, out_f32
    else:
        out_specs = (any_spec, any_spec)
        out_shape = (out_f32, jax.ShapeDtypeStruct((m, d), BF16))
        scratch.append(pltpu.VMEM((2, rc, d), BF16))
        sems.append(pltpu.SemaphoreType.DMA((2,)))
    return pl.pallas_call(
        kern,
        grid=(nblk, nf),
        in_specs=[any_spec, vec,
                  pl.BlockSpec((d, bf), lambda i, f: (0, f)),
                  pl.BlockSpec((d, bf), lambda i, f: (0, f)),
                  pl.BlockSpec((bf, d), lambda i, f: (f, 0)),
                  vec],
        out_specs=out_specs,
        out_shape=out_shape,
        scratch_shapes=scratch + sems,
        compiler_params=_params(("arbitrary", "arbitrary"), FFN_VMEM_LIMIT),
        name="ffn_final" if final_norm else "ffn",
    )(x, g_in.reshape(1, d), wg, wu, wd, g_out.reshape(1, d))


def _proj_glu_kernel(h_ref, wa_ref, wg_ref, o_ref):
    h = h_ref[...]
    a = jnp.dot(h, wa_ref[...].astype(BF16), preferred_element_type=F32)
    g = jnp.dot(h, wg_ref[...].astype(BF16), preferred_element_type=F32)
    o_ref[...] = a * jax.nn.sigmoid(g)


def _proj_glu(h, w_in, c_conv):
    m, d = h.shape
    bm = _block(m, 1024)
    bn = _block(c_conv, 256)
    nb = c_conv // bn
    return pl.pallas_call(
        _proj_glu_kernel,
        grid=(m // bm, nb),
        in_specs=[pl.BlockSpec((bm, d), lambda i, j: (i, 0)),
                  pl.BlockSpec((d, bn), lambda i, j: (0, j)),
                  pl.BlockSpec((d, bn), lambda i, j: (0, j + nb))],
        out_specs=pl.BlockSpec((bm, bn), lambda i, j: (i, j)),
        out_shape=jax.ShapeDtypeStruct((m, c_conv), F32),
        compiler_params=_params(("parallel", "arbitrary")),
        name="proj_glu",
    )(h, w_in, w_in)


PROJ_ROW_SPLIT = 4


def _row_parts(bm):
    rs = bm // PROJ_ROW_SPLIT if bm % (16 * PROJ_ROW_SPLIT) == 0 else bm
    return [slice(r0, r0 + rs) for r0 in range(0, bm, rs)]


def _proj_qk_kernel(h_ref, w_ref, g_ref, o_ref):
    w = w_ref[...].astype(BF16)
    for rows in _row_parts(h_ref.shape[0]):
        t = jnp.dot(h_ref[rows, :], w, preferred_element_type=F32)
        for hd in range(t.shape[1] // HEAD_DIM):
            cols = slice(hd * HEAD_DIM, (hd + 1) * HEAD_DIM)
            th = t[:, cols]
            ms = jnp.mean(th * th, axis=-1, keepdims=True)
            o_ref[rows, cols] = ((th * lax.rsqrt(ms + EPS)) * g_ref[:, cols]).astype(o_ref.dtype)


def _proj_qk(h, w_in, gains, col0):
    m, d = h.shape
    n = gains.shape[0]
    bm = _block(m, 1024)
    bn = _block(n, 512)
    off = col0 // bn
    return pl.pallas_call(
        _proj_qk_kernel,
        grid=(m // bm, n // bn),
        in_specs=[pl.BlockSpec((bm, d), lambda i, j: (i, 0)),
                  pl.BlockSpec((d, bn), lambda i, j: (0, j + off)),
                  pl.BlockSpec((1, bn), lambda i, j: (0, j))],
        out_specs=pl.BlockSpec((bm, bn), lambda i, j: (i, j)),
        out_shape=jax.ShapeDtypeStruct((m, n), BF16),
        compiler_params=_params(("parallel", "arbitrary")),
        name="proj_qk",
    )(h, w_in, gains.reshape(1, n))


def _proj_v_kernel(h_ref, w_ref, o_ref):
    o_ref[...] = jnp.dot(h_ref[...], w_ref[...].astype(BF16), preferred_element_type=F32).astype(o_ref.dtype)


def _proj_v(h, w_in, col0, n):
    m, d = h.shape
    bm = _block(m, 1024)
    bn = _block(n, 512)
    off = col0 // bn
    return pl.pallas_call(
        _proj_v_kernel,
        grid=(m // bm, n // bn),
        in_specs=[pl.BlockSpec((bm, d), lambda i, j: (i, 0)),
                  pl.BlockSpec((d, bn), lambda i, j: (0, j + off))],
        out_specs=pl.BlockSpec((bm, bn), lambda i, j: (i, j)),
        out_shape=jax.ShapeDtypeStruct((m, n), BF16),
        compiler_params=_params(("parallel", "arbitrary")),
        name="proj_v",
    )(h, w_in)


CONV_HALO = 16
CONV_LANES = 128


SUBLANES = 8
CONV_FIRST = CONV_HALO - CONV_K // 2
CONV_SHIFT_ROWS = SUBLANES * ((CONV_FIRST + CONV_K - 1) // SUBLANES)


def _conv_kernel(prev_ref, main_ref, next_ref, w_ref, b_ref, lg_ref, lb_ref, o_ref, xpad_ref, shift_ref, y_ref):
    t = pl.program_id(1)
    ts, c = main_ref.shape
    prev = prev_ref[...]
    nxt = next_ref[...]
    xpad_ref[0:CONV_HALO, :] = jnp.where(t == 0, jnp.zeros_like(prev), prev)
    xpad_ref[CONV_HALO:CONV_HALO + ts, :] = main_ref[...]
    xpad_ref[CONV_HALO + ts:, :] = jnp.where(t == pl.num_programs(1) - 1, jnp.zeros_like(nxt), nxt)
    n_shift = ts + CONV_SHIFT_ROWS

    def chunk(ci, carry):
        lanes = pl.ds(pl.multiple_of(ci * CONV_LANES, CONV_LANES), CONV_LANES)
        for s in range(1, SUBLANES):
            shift_ref[s - 1] = xpad_ref[s:s + n_shift, lanes]
        acc = jnp.zeros((ts, CONV_LANES), F32)
        for k in range(CONV_K):
            a, s = divmod(CONV_FIRST + k, SUBLANES)
            if s == 0:
                xk = xpad_ref[a * SUBLANES:a * SUBLANES + ts, lanes]
            else:
                xk = shift_ref[s - 1, a * SUBLANES:a * SUBLANES + ts, :]
            acc = acc + w_ref[k:k + 1, lanes] * xk
        y_ref[:, lanes] = acc + b_ref[:, lanes]
        return carry

    lax.fori_loop(0, c // CONV_LANES, chunk, 0)

    y = y_ref[...]
    mu = jnp.mean(y, axis=-1, keepdims=True)
    yc = y - mu
    var = jnp.mean(yc * yc, axis=-1, keepdims=True)
    z = (yc * lax.rsqrt(var + EPS)) * lg_ref[...] + lb_ref[...]
    o_ref[...] = (z * jax.nn.sigmoid(z)).astype(o_ref.dtype)


def _conv_module(u, seq, conv_w, conv_b, ln_g, ln_b):
    m, c = u.shape
    ts = _block(seq, 256)
    nt = seq // ts
    hb = ts // CONV_HALO
    nh = m // CONV_HALO
    last_h = seq // CONV_HALO - 1

    def prev_map(b, t):
        return (b * (seq // CONV_HALO) + jnp.maximum(t * hb - 1, 0), 0)

    def next_map(b, t):
        return (b * (seq // CONV_HALO) + jnp.minimum((t + 1) * hb, last_h), 0)

    del nh
    vec = lambda: pl.BlockSpec((1, c), lambda b, t: (0, 0))
    return pl.pallas_call(
        _conv_kernel,
        grid=(m // seq, nt),
        in_specs=[pl.BlockSpec((CONV_HALO, c), prev_map),
                  pl.BlockSpec((ts, c), lambda b, t: (b * nt + t, 0)),
                  pl.BlockSpec((CONV_HALO, c), next_map),
                  pl.BlockSpec((CONV_K, c), lambda b, t: (0, 0)),
                  vec(), vec(), vec()],
        out_specs=pl.BlockSpec((ts, c), lambda b, t: (b * nt + t, 0)),
        out_shape=jax.ShapeDtypeStruct((m, c), BF16),
        scratch_shapes=[pltpu.VMEM((ts + 2 * CONV_HALO, c), F32),
                        pltpu.VMEM((SUBLANES - 1, ts + CONV_SHIFT_ROWS, CONV_LANES), F32),
                        pltpu.VMEM((ts, c), F32)],
        compiler_params=_params(("parallel", "arbitrary")),
        name="conv_module",
    )(u, u, u, conv_w, conv_b.reshape(1, c), ln_g.reshape(1, c), ln_b.reshape(1, c))


N_DR = 2 * NA_ROWS - 1
N_DC = 2 * NA_COLS - 1


def _bias_tile_plan(rows):
    plan = {}
    for v, r0 in enumerate((0, Q_ROWS, rows - Q_ROWS)):
        ks = min(max(r0 - NA_ROWS // 2, 0), rows - KEY_ROWS)
        for g in range(Q_ROWS):
            r = r0 + g
            row_start = min(max(r - NA_ROWS // 2, 0), rows - NA_ROWS)
            for p in range(KEY_ROWS // 2):
                oks = [row_start <= ks + j < row_start + NA_ROWS for j in (2 * p, 2 * p + 1)]
                dr_left = ks + 2 * p - r + NA_ROWS - 1
                key = (dr_left if any(oks) else None, oks[0], oks[1])
                plan.setdefault(key, []).append((v, g, p))
    return plan


def _bias_table_kernel(rpb_ref, o_ref, *, rows):
    base = pl.program_id(0) * (N_DR * N_DC)
    shape = (GRID_W, 2 * GRID_W)
    lane = lax.broadcasted_iota(jnp.int32, shape, 1)
    qc = lax.broadcasted_iota(jnp.int32, shape, 0)
    right = lane >= GRID_W
    kc = jnp.where(right, lane - GRID_W, lane)
    delta = kc - qc + (NA_COLS - 1)
    win_start = jnp.clip(qc - NA_COLS // 2, 0, GRID_W - NA_COLS)
    col_ok = (kc >= win_start) & (kc < win_start + NA_COLS)
    for (dr_left, ok_l, ok_r), dests in _bias_tile_plan(rows).items():
        if dr_left is None:
            tile = jnp.full(shape, MASK_VALUE, F32)
        else:
            acc = jnp.zeros(shape, F32)
            for dc in range(N_DC):
                s_l = rpb_ref[base + dr_left * N_DC + dc] if ok_l else 0.0
                s_r = rpb_ref[base + (dr_left + 1) * N_DC + dc] if ok_r else 0.0
                acc = jnp.where(delta == dc, jnp.where(right, s_r, s_l), acc)
            ok = col_ok if (ok_l and ok_r) else (col_ok & right if ok_r else col_ok & ~right)
            tile = jnp.where(ok, acc, MASK_VALUE)
        for v, g, p in dests:
            o_ref[0, v, g * GRID_W:(g + 1) * GRID_W, p * 2 * GRID_W:(p + 1) * 2 * GRID_W] = tile


def _attn_bias_table(rpb, rows):
    n_heads = rpb.shape[0]
    assert rpb.shape[1:] == (N_DR, N_DC) and rows % Q_ROWS == 0 and rows >= KEY_ROWS + Q_ROWS
    blk = (1, 3, Q_ROWS * GRID_W, KEY_ROWS * GRID_W)
    return pl.pallas_call(
        functools.partial(_bias_table_kernel, rows=rows),
        grid=(n_heads,),
        in_specs=[pl.BlockSpec(memory_space=pltpu.SMEM)],
        out_specs=pl.BlockSpec(blk, lambda h: (h, 0, 0, 0)),
        out_shape=jax.ShapeDtypeStruct((n_heads,) + blk[1:], F32),
        compiler_params=_params(("parallel",)),
        name="bias_table",
    )(rpb.reshape(-1))


def _attn_kernel(q_ref, k_ref, v_ref, tbl_ref, o_ref, s_ref, p_ref, *, rows, heads):
    rb = pl.program_id(2)
    r0 = rb * Q_ROWS
    ks = jnp.clip(r0 - NA_ROWS // 2, 0, rows - KEY_ROWS)
    nk = KEY_ROWS * GRID_W
    keys = pl.ds(pl.multiple_of(ks * GRID_W, GRID_W), nk)
    head_cols = [slice(hd * HEAD_DIM, (hd + 1) * HEAD_DIM) for hd in range(heads)]
    for hd, cols in enumerate(head_cols):
        s = lax.dot_general(q_ref[:, cols], k_ref[keys, cols], (((1,), (1,)), ((), ())),
                            preferred_element_type=F32)
        s_ref[hd] = s + tbl_ref[hd, 0]
    for hd in range(heads):
        s = s_ref[hd]
        p_ref[hd] = jnp.exp(s - jnp.max(s, axis=-1, keepdims=True)).astype(p_ref.dtype)
    ones = jnp.ones((nk, HEAD_DIM), v_ref.dtype)
    for hd, cols in enumerate(head_cols):
        v_ext = jnp.concatenate([v_ref[keys, cols], ones], axis=1)
        o = jnp.dot(p_ref[hd], v_ext, preferred_element_type=F32)
        o_ref[:, cols] = (o[:, :HEAD_DIM] / o[:, HEAD_DIM:]).astype(o_ref.dtype)


def _attention(qk, v, tbl, batch, seq):
    m, c = v.shape
    rows = seq // GRID_W
    nrb = rows // Q_ROWS
    heads = min(8, c // HEAD_DIM)
    bc = heads * HEAD_DIM
    ngrp = c // bc
    bq = Q_ROWS * GRID_W

    def variant(rb):
        return jnp.where(rb == 0, 0, jnp.where(rb == nrb - 1, 2, 1))

    kern = functools.partial(_attn_kernel, rows=rows, heads=heads)
    return pl.pallas_call(
        kern,
        grid=(batch, ngrp, nrb),
        in_specs=[pl.BlockSpec((bq, bc), lambda b, g, r: (b * nrb + r, g)),
                  pl.BlockSpec((seq, bc), lambda b, g, r: (b, ngrp + g)),
                  pl.BlockSpec((seq, bc), lambda b, g, r: (b, g)),
                  pl.BlockSpec((heads, 1, bq, KEY_ROWS * GRID_W), lambda b, g, r: (g, variant(r), 0, 0))],
        out_specs=pl.BlockSpec((bq, bc), lambda b, g, r: (b * nrb + r, g)),
        out_shape=jax.ShapeDtypeStruct((m, c), BF16),
        scratch_shapes=[pltpu.VMEM((heads, bq, KEY_ROWS * GRID_W), F32),
                        pltpu.VMEM((heads, bq, KEY_ROWS * GRID_W), BF16)],
        compiler_params=_params(("parallel", "parallel", "arbitrary")),
        name="natten",
    )(qk, qk, v, tbl)


def _out_proj_kernel(x_ref, c_ref, a_ref, w_ref, o_ref):
    mixed = jnp.concatenate([c_ref[...], a_ref[...]], axis=1)
    o_ref[...] = x_ref[...] + jnp.dot(mixed, w_ref[...].astype(BF16), preferred_element_type=F32)


def _out_proj(x, conv_out, attn_out, w_out):
    m, d = x.shape
    cc = conv_out.shape[1]
    ca = attn_out.shape[1]
    assert w_out.shape[0] == cc + ca
    bm = _block(m, 1024)
    bn = _block(d, 512)
    return pl.pallas_call(
        _out_proj_kernel,
        grid=(m // bm, d // bn),
        in_specs=[pl.BlockSpec((bm, bn), lambda i, j: (i, j)),
                  pl.BlockSpec((bm, cc), lambda i, j: (i, 0)),
                  pl.BlockSpec((bm, ca), lambda i, j: (i, 0)),
                  pl.BlockSpec((cc + ca, bn), lambda i, j: (0, j))],
        out_specs=pl.BlockSpec((bm, bn), lambda i, j: (i, j)),
        out_shape=jax.ShapeDtypeStruct((m, d), F32),
        compiler_params=_params(("parallel", "arbitrary")),
        name="out_proj",
    )(x, conv_out, attn_out, w_out)


def kernel(x, g_ffn1, w1_gate, w1_up, w1_down, g_mix, w_in, conv_w, conv_b, conv_ln_g, conv_ln_b, q_norm_g, k_norm_g, rpb, w_out, g_ffn2, w2_gate, w2_up, w2_down, g_final):
    batch, seq, d = x.shape
    depth = g_ffn1.shape[0]
    c_conv = conv_w.shape[2]
    c_attn = w_out.shape[1] - c_conv
    n_heads = c_attn // HEAD_DIM
    rows = seq // GRID_W
    scale = HEAD_DIM ** -0.5

    xs = x.reshape(batch * seq, d)
    for l in range(depth):
        qk_gains = jnp.concatenate([jnp.tile(q_norm_g[l] * scale, n_heads), jnp.tile(k_norm_g[l], n_heads)])
        tbl = _attn_bias_table(rpb[l], rows)

        xs, h2 = _ffn(xs, g_ffn1[l], w1_gate[l], w1_up[l], w1_down[l], g_mix[l], final_norm=False)
        u = _proj_glu(h2, w_in[l], c_conv)
        qk = _proj_qk(h2, w_in[l], qk_gains, 2 * c_conv)
        v = _proj_v(h2, w_in[l], 2 * c_conv + 2 * c_attn, c_attn)
        conv_out = _conv_module(u, seq, conv_w[l], conv_b[l], conv_ln_g[l], conv_ln_b[l])
        attn_out = _attention(qk, v, tbl, batch, seq)
        xs = _out_proj(xs, conv_out, attn_out, w_out[l])

        xs = _ffn(xs, g_ffn2[l], w2_gate[l], w2_up[l], w2_down[l], g_final[l], final_norm=True)
    return xs.reshape(batch, seq, d)
```

```python
import functools

import jax
import jax.numpy as jnp
from jax import lax
from jax.experimental import pallas as pl
from jax.experimental.pallas import tpu as pltpu

F32 = jnp.float32
BF16 = jnp.bfloat16

EPS = 1e-6
HEAD_DIM = 128
CONV_K = 31
GRID_W = 64
NA_ROWS = 8
NA_COLS = 16
Q_ROWS = 4
KEY_ROWS = Q_ROWS + NA_ROWS
ATTN_HEADS_PER_STEP = 8
MASK_VALUE = -1e30

V7X_VMEM_LIMIT = 56 * 1024 * 1024
FFN_VMEM_LIMIT = 60 * 1024 * 1024


def _block(dim, pref):
    b = min(dim, pref)
    while dim % b:
        b //= 2
    return b


def _params(semantics, vmem_limit=V7X_VMEM_LIMIT):
    return pltpu.CompilerParams(dimension_semantics=semantics, vmem_limit_bytes=vmem_limit)


FFN_NORM_ROWS = 64


def _rms_rows(v, g):
    ms = jnp.mean(v * v, axis=-1, keepdims=True)
    return (v * lax.rsqrt(ms + EPS)) * g


def _ffn_kernel(*refs, bm, bn, nf, nblk, n_look, final_norm):
    if final_norm:
        (x_hbm, gin_ref, wg_ref, wu_ref, wd_ref, gout_ref, out_hbm,
         acc_ref, h_scr, xst_ref, sem_in, sem_out, sem_look) = refs
        hnext_hbm = hst_ref = sem_h = None
    else:
        (x_hbm, gin_ref, wg_ref, wu_ref, wd_ref, gout_ref, out_hbm, hnext_hbm,
         acc_ref, h_scr, xst_ref, hst_ref, sem_in, sem_out, sem_look, sem_h) = refs
    i = pl.program_id(0)
    f = pl.program_id(1)
    t = i * nf + f
    rows = pl.ds(pl.multiple_of(i * bm, bm), bm)
    d = acc_ref.shape[1]
    nc = d // bn
    rc = min(bm, FFN_NORM_ROWS)
    lr = bm // n_look

    def look_copy(blk, chunk, slot):
        src = x_hbm.at[pl.ds(pl.multiple_of(blk * bm + chunk * lr, lr), lr)]
        return pltpu.make_async_copy(src, xst_ref.at[slot], sem_look.at[slot])

    def look_target(blk_i, step_f):
        return jnp.minimum(blk_i + 1, nblk - 1), jnp.minimum(step_f, n_look - 1)

    def normalise_chunk(slot, h_slot, chunk):
        hn = _rms_rows(xst_ref[slot], gin_ref[...]).astype(BF16)
        h_scr[h_slot, pl.ds(pl.multiple_of(chunk * lr, lr), lr), :] = hn

    def hnext_copy(r, slot):
        dst = hnext_hbm.at[pl.ds(pl.multiple_of(i * bm + r * rc, rc), rc)]
        return pltpu.make_async_copy(hst_ref.at[slot], dst, sem_h.at[slot])

    def seed_copy(c):
        cols = pl.ds(c * bn, bn)
        return pltpu.make_async_copy(x_hbm.at[rows, cols], acc_ref.at[:, cols], sem_in.at[c])

    def col_writeback(c):
        cols = pl.ds(c * bn, bn)
        return pltpu.make_async_copy(acc_ref.at[:, cols], out_hbm.at[rows, cols], sem_out.at[c])

    def row_writeback(r):
        src = acc_ref.at[pl.ds(pl.multiple_of(r * rc, rc), rc)]
        dst = out_hbm.at[pl.ds(pl.multiple_of(i * bm + r * rc, rc), rc)]
        return pltpu.make_async_copy(src, dst, sem_out.at[r])

    def prologue():
        look_copy(0, 0, 0).start()

        def first_block_chunk(c, carry):
            slot = c % 2

            @pl.when(c + 1 < n_look)
            def _():
                look_copy(0, c + 1, 1 - slot).start()

            look_copy(0, c, slot).wait()
            normalise_chunk(slot, 0, c)
            return carry

        lax.fori_loop(0, n_look, first_block_chunk, 0)
        for t0 in range(min(2, nblk * nf)):
            look_copy(min(t0 // nf + 1, nblk - 1), min(t0 % nf, n_look - 1), t0 % 2).start()

    def step(first, last):
        if first:
            pl.when(i == 0)(prologue)
            for c in range(nc):
                seed_copy(c).start()
        slot = t % 2
        look_blk, look_chunk = look_target(i, f)
        look_copy(look_blk, look_chunk, slot).wait()
        h = h_scr[i % 2]
        gate = jnp.dot(h, wg_ref[...].astype(BF16), preferred_element_type=F32)
        up = jnp.dot(h, wu_ref[...].astype(BF16), preferred_element_type=F32)
        act = ((0.5 * (gate * jax.nn.sigmoid(gate))) * up).astype(BF16)
        for c in range(nc):
            cols = slice(c * bn, (c + 1) * bn)
            if first:
                seed_copy(c).wait()
            acc_ref[:, cols] += jnp.dot(act, wd_ref[:, cols].astype(BF16), preferred_element_type=F32)
            if last and not final_norm:
                col_writeback(c).start()
        normalise_chunk(slot, (i + 1) % 2, look_chunk)
        n_rows = bm // rc
        if last and final_norm:
            def norm_rows(r, carry):
                rs = pl.ds(pl.multiple_of(r * rc, rc), rc)
                acc_ref[rs, :] = _rms_rows(acc_ref[rs, :], gout_ref[...])
                row_writeback(r).start()
                return carry

            lax.fori_loop(0, n_rows, norm_rows, 0)

            def wait_rows(r, carry):
                row_writeback(r).wait()
                return carry

            lax.fori_loop(0, n_rows, wait_rows, 0)
        elif last:
            def emit_rows(r, carry):
                h_slot = r % 2

                @pl.when(r >= 2)
                def _():
                    hnext_copy(r - 2, h_slot).wait()

                rs = pl.ds(pl.multiple_of(r * rc, rc), rc)
                hst_ref[h_slot] = _rms_rows(acc_ref[rs, :], gout_ref[...]).astype(BF16)
                hnext_copy(r, h_slot).start()
                return carry

            lax.fori_loop(0, n_rows, emit_rows, 0)
            for r in range(max(n_rows - 2, 0), n_rows):
                hnext_copy(r, r % 2).wait()
            for c in range(nc):
                col_writeback(c).wait()

        t2 = t + 2

        @pl.when(t2 < nblk * nf)
        def _():
            blk2, chunk2 = look_target(lax.div(t2, nf), lax.rem(t2, nf))
            look_copy(blk2, chunk2, slot).start()

    if nf == 1:
        step(True, True)
    else:
        pl.when(f == 0)(lambda: step(True, False))
        pl.when((f > 0) & (f < nf - 1))(lambda: step(False, False))
        pl.when(f == nf - 1)(lambda: step(False, True))


def _ffn(x, g_in, wg, wu, wd, g_out, final_norm):
    m, d = x.shape
    ff = wg.shape[1]
    bm = _block(m, 1024)
    bf = _block(ff, 256)
    bn = _block(d, 512)
    assert wd.shape == (ff, d) and wu.shape == wg.shape == (d, ff)
    nf = ff // bf
    nblk = m // bm
    n_look = 1
    while 2 * n_look <= nf and bm % (2 * n_look) == 0 and (bm // (2 * n_look)) % 16 == 0:
        n_look *= 2
    rc = min(bm, FFN_NORM_ROWS)
    n_out_sems = max(d // bn, bm // rc)
    kern = functools.partial(_ffn_kernel, bm=bm, bn=bn, nf=nf, nblk=nblk, n_look=n_look, final_norm=final_norm)
    vec = pl.BlockSpec((1, d), lambda i, f: (0, 0))
    any_spec = pl.BlockSpec(memory_space=pl.ANY)
    out_f32 = jax.ShapeDtypeStruct((m, d), F32)
    scratch = [pltpu.VMEM((bm, d), F32),
               pltpu.VMEM((2, bm, d), BF16),
               pltpu.VMEM((2, bm // n_look, d), F32)]
    sems = [pltpu.SemaphoreType.DMA((d // bn,)), pltpu.SemaphoreType.DMA((n_out_sems,)),
            pltpu.SemaphoreType.DMA((2,))]
    if final_norm:
        out_specs, out_shape = any_spec, out_f32
    else:
        out_specs = (any_spec, any_spec)
        out_shape = (out_f32, jax.ShapeDtypeStruct((m, d), BF16))
        scratch.append(pltpu.VMEM((2, rc, d), BF16))
        sems.append(pltpu.SemaphoreType.DMA((2,)))
    return pl.pallas_call(
        kern,
        grid=(nblk, nf),
        in_specs=[any_spec, vec,
                  pl.BlockSpec((d, bf), lambda i, f: (0, f)),
                  pl.BlockSpec((d, bf), lambda i, f: (0, f)),
                  pl.BlockSpec((bf, d), lambda i, f: (f, 0)),
                  vec],
        out_specs=out_specs,
        out_shape=out_shape,
        scratch_shapes=scratch + sems,
        compiler_params=_params(("arbitrary", "arbitrary"), FFN_VMEM_LIMIT),
        name="ffn_final" if final_norm else "ffn",
    )(x, g_in.reshape(1, d), wg, wu, wd, g_out.reshape(1, d))


def _proj_glu_kernel(h_ref, wa_ref, wg_ref, o_ref):
    h = h_ref[...]
    a = jnp.dot(h, wa_ref[...].astype(BF16), preferred_element_type=F32)
    g = jnp.dot(h, wg_ref[...].astype(BF16), preferred_element_type=F32)
    o_ref[...] = a * jax.nn.sigmoid(g)


def _proj_glu(h, w_in, c_conv):
    m, d = h.shape
    bm = _block(m, 1024)
    bn = _block(c_conv, 256)
    nb = c_conv // bn
    return pl.pallas_call(
        _proj_glu_kernel,
        grid=(m // bm, nb),
        in_specs=[pl.BlockSpec((bm, d), lambda i, j: (i, 0)),
                  pl.BlockSpec((d, bn), lambda i, j: (0, j)),
                  pl.BlockSpec((d, bn), lambda i, j: (0, j + nb))],
        out_specs=pl.BlockSpec((bm, bn), lambda i, j: (i, j)),
        out_shape=jax.ShapeDtypeStruct((m, c_conv), F32),
        compiler_params=_params(("parallel", "arbitrary")),
        name="proj_glu",
    )(h, w_in, w_in)


PROJ_ROW_SPLIT = 4


def _row_parts(bm):
    rs = bm // PROJ_ROW_SPLIT if bm % (16 * PROJ_ROW_SPLIT) == 0 else bm
    return [slice(r0, r0 + rs) for r0 in range(0, bm, rs)]


def _proj_qk_kernel(h_ref, w_ref, g_ref, o_ref):
    w = w_ref[...].astype(BF16)
    for rows in _row_parts(h_ref.shape[0]):
        t = jnp.dot(h_ref[rows, :], w, preferred_element_type=F32)
        for hd in range(t.shape[1] // HEAD_DIM):
            cols = slice(hd * HEAD_DIM, (hd + 1) * HEAD_DIM)
            th = t[:, cols]
            ms = jnp.mean(th * th, axis=-1, keepdims=True)
            o_ref[rows, cols] = ((th * lax.rsqrt(ms + EPS)) * g_ref[:, cols]).astype(o_ref.dtype)


def _proj_qk(h, w_in, gains, col0):
    m, d = h.shape
    n = gains.shape[0]
    bm = _block(m, 1024)
    bn = _block(n, 512)
    assert col0 % bn == 0 and bn % HEAD_DIM == 0
    off = col0 // bn
    return pl.pallas_call(
        _proj_qk_kernel,
        grid=(m // bm, n // bn),
        in_specs=[pl.BlockSpec((bm, d), lambda i, j: (i, 0)),
                  pl.BlockSpec((d, bn), lambda i, j: (0, j + off)),
                  pl.BlockSpec((1, bn), lambda i, j: (0, j))],
        out_specs=pl.BlockSpec((bm, bn), lambda i, j: (i, j)),
        out_shape=jax.ShapeDtypeStruct((m, n), BF16),
        compiler_params=_params(("parallel", "arbitrary")),
        name="proj_qk",
    )(h, w_in, gains.reshape(1, n))


def _proj_v_kernel(h_ref, w_ref, o_ref):
    o_ref[...] = jnp.dot(h_ref[...], w_ref[...].astype(BF16), preferred_element_type=F32).astype(o_ref.dtype)


def _proj_v(h, w_in, col0, n):
    m, d = h.shape
    bm = _block(m, 1024)
    bn = _block(n, 512)
    assert col0 % bn == 0
    off = col0 // bn
    return pl.pallas_call(
        _proj_v_kernel,
        grid=(m // bm, n // bn),
        in_specs=[pl.BlockSpec((bm, d), lambda i, j: (i, 0)),
                  pl.BlockSpec((d, bn), lambda i, j: (0, j + off))],
        out_specs=pl.BlockSpec((bm, bn), lambda i, j: (i, j)),
        out_shape=jax.ShapeDtypeStruct((m, n), BF16),
        compiler_params=_params(("parallel", "arbitrary")),
        name="proj_v",
    )(h, w_in)


SUBLANES = 8
CONV_HALO = 16
CONV_LANES = 128
CONV_FIRST = CONV_HALO - CONV_K // 2
CONV_SHIFT_ROWS = SUBLANES * ((CONV_FIRST + CONV_K - 1) // SUBLANES)


def _conv_kernel(prev_ref, main_ref, next_ref, w_ref, b_ref, lg_ref, lb_ref, o_ref, xpad_ref, shift_ref, y_ref):
    t = pl.program_id(1)
    ts, c = main_ref.shape
    prev = prev_ref[...]
    nxt = next_ref[...]
    xpad_ref[0:CONV_HALO, :] = jnp.where(t == 0, jnp.zeros_like(prev), prev)
    xpad_ref[CONV_HALO:CONV_HALO + ts, :] = main_ref[...]
    xpad_ref[CONV_HALO + ts:, :] = jnp.where(t == pl.num_programs(1) - 1, jnp.zeros_like(nxt), nxt)
    n_shift = ts + CONV_SHIFT_ROWS

    def chunk(ci, carry):
        lanes = pl.ds(pl.multiple_of(ci * CONV_LANES, CONV_LANES), CONV_LANES)
        for s in range(1, SUBLANES):
            shift_ref[s - 1] = xpad_ref[s:s + n_shift, lanes]
        acc = jnp.zeros((ts, CONV_LANES), F32)
        for k in range(CONV_K):
            a, s = divmod(CONV_FIRST + k, SUBLANES)
            if s == 0:
                xk = xpad_ref[a * SUBLANES:a * SUBLANES + ts, lanes]
            else:
                xk = shift_ref[s - 1, a * SUBLANES:a * SUBLANES + ts, :]
            acc = acc + w_ref[k:k + 1, lanes] * xk
        y_ref[:, lanes] = acc + b_ref[:, lanes]
        return carry

    lax.fori_loop(0, c // CONV_LANES, chunk, 0)

    y = y_ref[...]
    mu = jnp.mean(y, axis=-1, keepdims=True)
    yc = y - mu
    var = jnp.mean(yc * yc, axis=-1, keepdims=True)
    z = (yc * lax.rsqrt(var + EPS)) * lg_ref[...] + lb_ref[...]
    o_ref[...] = (z * jax.nn.sigmoid(z)).astype(o_ref.dtype)


def _conv_module(u, seq, conv_w, conv_b, ln_g, ln_b):
    m, c = u.shape
    ts = _block(seq, 256)
    assert ts % CONV_HALO == 0 and c % CONV_LANES == 0 and m % seq == 0
    nt = seq // ts
    hb = ts // CONV_HALO
    last_h = seq // CONV_HALO - 1

    def prev_map(b, t):
        return (b * (seq // CONV_HALO) + jnp.maximum(t * hb - 1, 0), 0)

    def next_map(b, t):
        return (b * (seq // CONV_HALO) + jnp.minimum((t + 1) * hb, last_h), 0)

    vec = lambda: pl.BlockSpec((1, c), lambda b, t: (0, 0))
    return pl.pallas_call(
        _conv_kernel,
        grid=(m // seq, nt),
        in_specs=[pl.BlockSpec((CONV_HALO, c), prev_map),
                  pl.BlockSpec((ts, c), lambda b, t: (b * nt + t, 0)),
                  pl.BlockSpec((CONV_HALO, c), next_map),
                  pl.BlockSpec((CONV_K, c), lambda b, t: (0, 0)),
                  vec(), vec(), vec()],
        out_specs=pl.BlockSpec((ts, c), lambda b, t: (b * nt + t, 0)),
        out_shape=jax.ShapeDtypeStruct((m, c), BF16),
        scratch_shapes=[pltpu.VMEM((ts + 2 * CONV_HALO, c), F32),
                        pltpu.VMEM((SUBLANES - 1, ts + CONV_SHIFT_ROWS, CONV_LANES), F32),
                        pltpu.VMEM((ts, c), F32)],
        compiler_params=_params(("parallel", "arbitrary")),
        name="conv_module",
    )(u, u, u, conv_w, conv_b.reshape(1, c), ln_g.reshape(1, c), ln_b.reshape(1, c))


N_DR = 2 * NA_ROWS - 1
N_DC = 2 * NA_COLS - 1


def _bias_tile_plan(rows):
    plan = {}
    for v, r0 in enumerate((0, Q_ROWS, rows - Q_ROWS)):
        ks = min(max(r0 - NA_ROWS // 2, 0), rows - KEY_ROWS)
        for g in range(Q_ROWS):
            r = r0 + g
            row_start = min(max(r - NA_ROWS // 2, 0), rows - NA_ROWS)
            for p in range(KEY_ROWS // 2):
                oks = [row_start <= ks + j < row_start + NA_ROWS for j in (2 * p, 2 * p + 1)]
                dr_left = ks + 2 * p - r + NA_ROWS - 1
                key = (dr_left if any(oks) else None, oks[0], oks[1])
                plan.setdefault(key, []).append((v, g, p))
    return plan


def _bias_table_kernel(rpb_ref, o_ref, *, rows):
    base = pl.program_id(0) * (N_DR * N_DC)
    shape = (GRID_W, 2 * GRID_W)
    lane = lax.broadcasted_iota(jnp.int32, shape, 1)
    qc = lax.broadcasted_iota(jnp.int32, shape, 0)
    right = lane >= GRID_W
    kc = jnp.where(right, lane - GRID_W, lane)
    delta = kc - qc + (NA_COLS - 1)
    win_start = jnp.clip(qc - NA_COLS // 2, 0, GRID_W - NA_COLS)
    col_ok = (kc >= win_start) & (kc < win_start + NA_COLS)
    for (dr_left, ok_l, ok_r), dests in _bias_tile_plan(rows).items():
        if dr_left is None:
            tile = jnp.full(shape, MASK_VALUE, F32)
        else:
            acc = jnp.zeros(shape, F32)
            for dc in range(N_DC):
                s_l = rpb_ref[base + dr_left * N_DC + dc] if ok_l else 0.0
                s_r = rpb_ref[base + (dr_left + 1) * N_DC + dc] if ok_r else 0.0
                acc = jnp.where(delta == dc, jnp.where(right, s_r, s_l), acc)
            ok = col_ok if (ok_l and ok_r) else (col_ok & right if ok_r else col_ok & ~right)
            tile = jnp.where(ok, acc, MASK_VALUE)
        for v, g, p in dests:
            o_ref[0, v, g * GRID_W:(g + 1) * GRID_W, p * 2 * GRID_W:(p + 1) * 2 * GRID_W] = tile


def _attn_bias_table(rpb, rows):
    n_heads = rpb.shape[0]
    assert rpb.shape[1:] == (N_DR, N_DC) and rows % Q_ROWS == 0 and rows >= KEY_ROWS + Q_ROWS
    blk = (1, 3, Q_ROWS * GRID_W, KEY_ROWS * GRID_W)
    return pl.pallas_call(
        functools.partial(_bias_table_kernel, rows=rows),
        grid=(n_heads,),
        in_specs=[pl.BlockSpec(memory_space=pltpu.SMEM)],
        out_specs=pl.BlockSpec(blk, lambda h: (h, 0, 0, 0)),
        out_shape=jax.ShapeDtypeStruct((n_heads,) + blk[1:], F32),
        compiler_params=_params(("parallel",)),
        name="bias_table",
    )(rpb.reshape(-1))


def _attn_kernel(q_ref, k_ref, v_ref, tbl_ref, o_ref, s_ref, p_ref, *, rows, heads):
    rb = pl.program_id(2)
    r0 = rb * Q_ROWS
    ks = jnp.clip(r0 - NA_ROWS // 2, 0, rows - KEY_ROWS)
    nk = KEY_ROWS * GRID_W
    keys = pl.ds(pl.multiple_of(ks * GRID_W, GRID_W), nk)
    head_cols = [slice(hd * HEAD_DIM, (hd + 1) * HEAD_DIM) for hd in range(heads)]
    for hd, cols in enumerate(head_cols):
        s = lax.dot_general(q_ref[:, cols], k_ref[keys, cols], (((1,), (1,)), ((), ())),
                            preferred_element_type=F32)
        s_ref[hd] = s + tbl_ref[hd, 0]
    for hd in range(heads):
        s = s_ref[hd]
        p_ref[hd] = jnp.exp(s - jnp.max(s, axis=-1, keepdims=True)).astype(p_ref.dtype)
    ones = jnp.ones((nk, HEAD_DIM), v_ref.dtype)
    for hd, cols in enumerate(head_cols):
        v_ext = jnp.concatenate([v_ref[keys, cols], ones], axis=1)
        o = jnp.dot(p_ref[hd], v_ext, preferred_element_type=F32)
        o_ref[:, cols] = (o[:, :HEAD_DIM] / o[:, HEAD_DIM:]).astype(o_ref.dtype)


def _attention(qk, v, tbl, batch, seq):
    m, c = v.shape
    rows = seq // GRID_W
    nrb = rows // Q_ROWS
    heads = min(ATTN_HEADS_PER_STEP, c // HEAD_DIM)
    assert rows % Q_ROWS == 0 and c % (heads * HEAD_DIM) == 0
    bc = heads * HEAD_DIM
    ngrp = c // bc
    bq = Q_ROWS * GRID_W

    def variant(rb):
        return jnp.where(rb == 0, 0, jnp.where(rb == nrb - 1, 2, 1))

    kern = functools.partial(_attn_kernel, rows=rows, heads=heads)
    return pl.pallas_call(
        kern,
        grid=(batch, ngrp, nrb),
        in_specs=[pl.BlockSpec((bq, bc), lambda b, g, r: (b * nrb + r, g)),
                  pl.BlockSpec((seq, bc), lambda b, g, r: (b, ngrp + g)),
                  pl.BlockSpec((seq, bc), lambda b, g, r: (b, g)),
                  pl.BlockSpec((heads, 1, bq, KEY_ROWS * GRID_W), lambda b, g, r: (g, variant(r), 0, 0))],
        out_specs=pl.BlockSpec((bq, bc), lambda b, g, r: (b * nrb + r, g)),
        out_shape=jax.ShapeDtypeStruct((m, c), BF16),
        scratch_shapes=[pltpu.VMEM((heads, bq, KEY_ROWS * GRID_W), F32),
                        pltpu.VMEM((heads, bq, KEY_ROWS * GRID_W), BF16)],
        compiler_params=_params(("parallel", "parallel", "arbitrary")),
        name="natten",
    )(qk, qk, v, tbl)


def _out_proj_kernel(x_ref, c_ref, a_ref, w_ref, o_ref):
    mixed = jnp.concatenate([c_ref[...], a_ref[...]], axis=1)
    o_ref[...] = x_ref[...] + jnp.dot(mixed, w_ref[...].astype(BF16), preferred_element_type=F32)


def _out_proj(x, conv_out, attn_out, w_out):
    m, d = x.shape
    cc = conv_out.shape[1]
    ca = attn_out.shape[1]
    assert w_out.shape[0] == cc + ca
    bm = _block(m, 1024)
    bn = _block(d, 512)
    return pl.pallas_call(
        _out_proj_kernel,
        grid=(m // bm, d // bn),
        in_specs=[pl.BlockSpec((bm, bn), lambda i, j: (i, j)),
                  pl.BlockSpec((bm, cc), lambda i, j: (i, 0)),
                  pl.BlockSpec((bm, ca), lambda i, j: (i, 0)),
                  pl.BlockSpec((cc + ca, bn), lambda i, j: (0, j))],
        out_specs=pl.BlockSpec((bm, bn), lambda i, j: (i, j)),
        out_shape=jax.ShapeDtypeStruct((m, d), F32),
        compiler_params=_params(("parallel", "arbitrary")),
        name="out_proj",
    )(x, conv_out, attn_out, w_out)


def kernel(x, g_ffn1, w1_gate, w1_up, w1_down, g_mix, w_in, conv_w, conv_b, conv_ln_g, conv_ln_b, q_norm_g, k_norm_g, rpb, w_out, g_ffn2, w2_gate, w2_up, w2_down, g_final):
    batch, seq, d = x.shape
    depth = g_ffn1.shape[0]
    c_conv = conv_w.shape[2]
    c_attn = w_out.shape[1] - c_conv
    n_heads = c_attn // HEAD_DIM
    rows = seq // GRID_W
    scale = HEAD_DIM ** -0.5

    xs = x.reshape(batch * seq, d)
    for l in range(depth):
        qk_gains = jnp.concatenate([jnp.tile(q_norm_g[l] * scale, n_heads), jnp.tile(k_norm_g[l], n_heads)])
        tbl = _attn_bias_table(rpb[l], rows)

        xs, h2 = _ffn(xs, g_ffn1[l], w1_gate[l], w1_up[l], w1_down[l], g_mix[l], final_norm=False)
        u = _proj_glu(h2, w_in[l], c_conv)
        qk = _proj_qk(h2, w_in[l], qk_gains, 2 * c_conv)
        v = _proj_v(h2, w_in[l], 2 * c_conv + 2 * c_attn, c_attn)
        conv_out = _conv_module(u, seq, conv_w[l], conv_b[l], conv_ln_g[l], conv_ln_b[l])
        attn_out = _attention(qk, v, tbl, batch, seq)
        xs = _out_proj(xs, conv_out, attn_out, w_out[l])

        xs = _ffn(xs, g_ffn2[l], w2_gate[l], w2_up[l], w2_down[l], g_final[l], final_norm=True)
    return xs.reshape(batch, seq, d)
```

```python
import functools

import jax
import jax.numpy as jnp
from jax import lax
from jax.experimental import pallas as pl
from jax.experimental.pallas import tpu as pltpu

F32 = jnp.float32
BF16 = jnp.bfloat16

EPS = 1e-6
HEAD_DIM = 128
CONV_K = 31
GRID_W = 64
NA_ROWS = 8
NA_COLS = 16
Q_ROWS = 4
KEY_ROWS = Q_ROWS + NA_ROWS
ATTN_HEADS_PER_STEP = 8
MASK_VALUE = -1e30

V7X_VMEM_LIMIT = 56 * 1024 * 1024
FFN_VMEM_LIMIT = 60 * 1024 * 1024


def _block(dim, pref):
    b = min(dim, pref)
    while dim % b:
        b //= 2
    return b


def _params(semantics, vmem_limit=V7X_VMEM_LIMIT):
    return pltpu.CompilerParams(dimension_semantics=semantics, vmem_limit_bytes=vmem_limit)


FFN_NORM_ROWS = 64


def _rms_rows(v, g):
    ms = jnp.mean(v * v, axis=-1, keepdims=True)
    return (v * lax.rsqrt(ms + EPS)) * g


def _ffn_kernel(*refs, bm, bn, nf, nblk, n_look, final_norm):
    if final_norm:
        (x_hbm, gin_ref, wg_ref, wu_ref, wd_ref, gout_ref, out_hbm,
         acc_ref, h_scr, xst_ref, sem_in, sem_out, sem_look) = refs
        hnext_hbm = hst_ref = sem_h = None
    else:
        (x_hbm, gin_ref, wg_ref, wu_ref, wd_ref, gout_ref, out_hbm, hnext_hbm,
         acc_ref, h_scr, xst_ref, hst_ref, sem_in, sem_out, sem_look, sem_h) = refs
    i = pl.program_id(0)
    f = pl.program_id(1)
    t = i * nf + f
    rows = pl.ds(pl.multiple_of(i * bm, bm), bm)
    d = acc_ref.shape[1]
    nc = d // bn
    rc = min(bm, FFN_NORM_ROWS)
    lr = bm // n_look

    def look_copy(blk, chunk, slot):
        src = x_hbm.at[pl.ds(pl.multiple_of(blk * bm + chunk * lr, lr), lr)]
        return pltpu.make_async_copy(src, xst_ref.at[slot], sem_look.at[slot])

    def look_target(blk_i, step_f):
        return jnp.minimum(blk_i + 1, nblk - 1), jnp.minimum(step_f, n_look - 1)

    def normalise_chunk(slot, h_slot, chunk):
        hn = _rms_rows(xst_ref[slot], gin_ref[...]).astype(BF16)
        h_scr[h_slot, pl.ds(pl.multiple_of(chunk * lr, lr), lr), :] = hn

    def hnext_copy(r, slot):
        dst = hnext_hbm.at[pl.ds(pl.multiple_of(i * bm + r * rc, rc), rc)]
        return pltpu.make_async_copy(hst_ref.at[slot], dst, sem_h.at[slot])

    def seed_copy(c):
        cols = pl.ds(c * bn, bn)
        return pltpu.make_async_copy(x_hbm.at[rows, cols], acc_ref.at[:, cols], sem_in.at[c])

    def col_writeback(c):
        cols = pl.ds(c * bn, bn)
        return pltpu.make_async_copy(acc_ref.at[:, cols], out_hbm.at[rows, cols], sem_out.at[c])

    def row_writeback(r):
        src = acc_ref.at[pl.ds(pl.multiple_of(r * rc, rc), rc)]
        dst = out_hbm.at[pl.ds(pl.multiple_of(i * bm + r * rc, rc), rc)]
        return pltpu.make_async_copy(src, dst, sem_out.at[r])

    def prologue():
        look_copy(0, 0, 0).start()

        def first_block_chunk(c, carry):
            slot = c % 2

            @pl.when(c + 1 < n_look)
            def _():
                look_copy(0, c + 1, 1 - slot).start()

            look_copy(0, c, slot).wait()
            normalise_chunk(slot, 0, c)
            return carry

        lax.fori_loop(0, n_look, first_block_chunk, 0)
        for t0 in range(min(2, nblk * nf)):
            look_copy(min(t0 // nf + 1, nblk - 1), min(t0 % nf, n_look - 1), t0 % 2).start()

    def step(first, last):
        if first:
            pl.when(i == 0)(prologue)
            for c in range(nc):
                seed_copy(c).start()
        slot = t % 2
        look_blk, look_chunk = look_target(i, f)
        look_copy(look_blk, look_chunk, slot).wait()
        h = h_scr[i % 2]
        gate = jnp.dot(h, wg_ref[...].astype(BF16), preferred_element_type=F32)
        up = jnp.dot(h, wu_ref[...].astype(BF16), preferred_element_type=F32)
        act = ((0.5 * (gate * jax.nn.sigmoid(gate))) * up).astype(BF16)
        for c in range(nc):
            cols = slice(c * bn, (c + 1) * bn)
            if first:
                seed_copy(c).wait()
            acc_ref[:, cols] += jnp.dot(act, wd_ref[:, cols].astype(BF16), preferred_element_type=F32)
            if last and not final_norm:
                col_writeback(c).start()
        normalise_chunk(slot, (i + 1) % 2, look_chunk)
        n_rows = bm // rc
        if last and final_norm:
            def norm_rows(r, carry):
                rs = pl.ds(pl.multiple_of(r * rc, rc), rc)
                acc_ref[rs, :] = _rms_rows(acc_ref[rs, :], gout_ref[...])
                row_writeback(r).start()
                return carry

            lax.fori_loop(0, n_rows, norm_rows, 0)

            def wait_rows(r, carry):
                row_writeback(r).wait()
                return carry

            lax.fori_loop(0, n_rows, wait_rows, 0)
        elif last:
            def emit_rows(r, carry):
                h_slot = r % 2

                @pl.when(r >= 2)
                def _():
                    hnext_copy(r - 2, h_slot).wait()

                rs = pl.ds(pl.multiple_of(r * rc, rc), rc)
                hst_ref[h_slot] = _rms_rows(acc_ref[rs, :], gout_ref[...]).astype(BF16)
                hnext_copy(r, h_slot).start()
                return carry

            lax.fori_loop(0, n_rows, emit_rows, 0)
            for r in range(max(n_rows - 2, 0), n_rows):
                hnext_copy(r, r % 2).wait()
            for c in range(nc):
                col_writeback(c).wait()

        t2 = t + 2

        @pl.when(t2 < nblk * nf)
        def _():
            blk2, chunk2 = look_target(lax.div(t2, nf), lax.rem(t2, nf))
            look_copy(blk2, chunk2, slot).start()

    if nf == 1:
        step(True, True)
    else:
        pl.when(f == 0)(lambda: step(True, False))
        pl.when((f > 0) & (f < nf - 1))(lambda: step(False, False))
        pl.when(f == nf - 1)(lambda: step(False, True))


def _ffn(x, g_in, wg, wu, wd, g_out, final_norm):
    m, d = x.shape
    ff = wg.shape[1]
    bm = _block(m, 1024)
    bf = _block(ff, 256)
    bn = _block(d, 512)
    assert wd.shape == (ff, d) and wu.shape == wg.shape == (d, ff)
    nf = ff // bf
    nblk = m // bm
    n_look = 1
    while 2 * n_look <= nf and bm % (2 * n_look) == 0 and (bm // (2 * n_look)) % 16 == 0:
        n_look *= 2
    rc = min(bm, FFN_NORM_ROWS)
    n_out_sems = max(d // bn, bm // rc)
    kern = functools.partial(_ffn_kernel, bm=bm, bn=bn, nf=nf, nblk=nblk, n_look=n_look, final_norm=final_norm)
    vec = pl.BlockSpec((1, d), lambda i, f: (0, 0))
    any_spec = pl.BlockSpec(memory_space=pl.ANY)
    out_f32 = jax.ShapeDtypeStruct((m, d), F32)
    scratch = [pltpu.VMEM((bm, d), F32),
               pltpu.VMEM((2, bm, d), BF16),
               pltpu.VMEM((2, bm // n_look, d), F32)]
    sems = [pltpu.SemaphoreType.DMA((d // bn,)), pltpu.SemaphoreType.DMA((n_out_sems,)),
            pltpu.SemaphoreType.DMA((2,))]
    if final_norm:
        out_specs, out_shape = any_spec, out_f32
    else:
        out_specs = (any_spec, any_spec)
        out_shape = (out_f32, jax.ShapeDtypeStruct((m, d), BF16))
        scratch.append(pltpu.VMEM((2, rc, d), BF16))
        sems.append(pltpu.SemaphoreType.DMA((2,)))
    return pl.pallas_call(
        kern,
        grid=(nblk, nf),
        in_specs=[any_spec, vec,
                  pl.BlockSpec((d, bf), lambda i, f: (0, f)),
                  pl.BlockSpec((d, bf), lambda i, f: (0, f)),
                  pl.BlockSpec((bf, d), lambda i, f: (f, 0)),
                  vec],
        out_specs=out_specs,
        out_shape=out_shape,
        scratch_shapes=scratch + sems,
        compiler_params=_params(("arbitrary", "arbitrary"), FFN_VMEM_LIMIT),
        name="ffn_final" if final_norm else "ffn",
    )(x, g_in.reshape(1, d), wg, wu, wd, g_out.reshape(1, d))


def _proj_glu_kernel(h_ref, wa_ref, wg_ref, o_ref):
    h = h_ref[...]
    a = jnp.dot(h, wa_ref[...].astype(BF16), preferred_element_type=F32)
    g = jnp.dot(h, wg_ref[...].astype(BF16), preferred_element_type=F32)
    o_ref[...] = a * jax.nn.sigmoid(g)


def _proj_glu(h, w_in, c_conv):
    m, d = h.shape
    bm = _block(m, 1024)
    bn = _block(c_conv, 256)
    nb = c_conv // bn
    return pl.pallas_call(
        _proj_glu_kernel,
        grid=(m // bm, nb),
        in_specs=[pl.BlockSpec((bm, d), lambda i, j: (i, 0)),
                  pl.BlockSpec((d, bn), lambda i, j: (0, j)),
                  pl.BlockSpec((d, bn), lambda i, j: (0, j + nb))],
        out_specs=pl.BlockSpec((bm, bn), lambda i, j: (i, j)),
        out_shape=jax.ShapeDtypeStruct((m, c_conv), F32),
        compiler_params=_params(("parallel", "arbitrary")),
        name="proj_glu",
    )(h, w_in, w_in)


PROJ_ROW_SPLIT = 4


def _row_parts(bm):
    rs = bm // PROJ_ROW_SPLIT if bm % (16 * PROJ_ROW_SPLIT) == 0 else bm
    return [slice(r0, r0 + rs) for r0 in range(0, bm, rs)]


def _proj_qk_kernel(h_ref, w_ref, g_ref, o_ref):
    w = w_ref[...].astype(BF16)
    for rows in _row_parts(h_ref.shape[0]):
        t = jnp.dot(h_ref[rows, :], w, preferred_element_type=F32)
        for hd in range(t.shape[1] // HEAD_DIM):
            cols = slice(hd * HEAD_DIM, (hd + 1) * HEAD_DIM)
            th = t[:, cols]
            ms = jnp.mean(th * th, axis=-1, keepdims=True)
            o_ref[rows, cols] = ((th * lax.rsqrt(ms + EPS)) * g_ref[:, cols]).astype(o_ref.dtype)


def _proj_qk(h, w_in, gains, col0):
    m, d = h.shape
    n = gains.shape[0]
    bm = _block(m, 1024)
    bn = _block(n, 512)
    assert col0 % bn == 0 and bn % HEAD_DIM == 0
    off = col0 // bn
    return pl.pallas_call(
        _proj_qk_kernel,
        grid=(m // bm, n // bn),
        in_specs=[pl.BlockSpec((bm, d), lambda i, j: (i, 0)),
                  pl.BlockSpec((d, bn), lambda i, j: (0, j + off)),
                  pl.BlockSpec((1, bn), lambda i, j: (0, j))],
        out_specs=pl.BlockSpec((bm, bn), lambda i, j: (i, j)),
        out_shape=jax.ShapeDtypeStruct((m, n), BF16),
        compiler_params=_params(("parallel", "arbitrary")),
        name="proj_qk",
    )(h, w_in, gains.reshape(1, n))


def _proj_v_kernel(h_ref, w_ref, o_ref):
    o_ref[...] = jnp.dot(h_ref[...], w_ref[...].astype(BF16), preferred_element_type=F32).astype(o_ref.dtype)


def _proj_v(h, w_in, col0, n):
    m, d = h.shape
    bm = _block(m, 1024)
    bn = _block(n, 512)
    assert col0 % bn == 0
    off = col0 // bn
    return pl.pallas_call(
        _proj_v_kernel,
        grid=(m // bm, n // bn),
        in_specs=[pl.BlockSpec((bm, d), lambda i, j: (i, 0)),
                  pl.BlockSpec((d, bn), lambda i, j: (0, j + off))],
        out_specs=pl.BlockSpec((bm, bn), lambda i, j: (i, j)),
        out_shape=jax.ShapeDtypeStruct((m, n), BF16),
        compiler_params=_params(("parallel", "arbitrary")),
        name="proj_v",
    )(h, w_in)


SUBLANES = 8
CONV_HALO = 16
CONV_LANES = 128
CONV_FIRST = CONV_HALO - CONV_K // 2
CONV_SHIFT_ROWS = SUBLANES * ((CONV_FIRST + CONV_K - 1) // SUBLANES)


def _conv_kernel(prev_ref, main_ref, next_ref, w_ref, b_ref, lg_ref, lb_ref, o_ref, xpad_ref, shift_ref, y_ref):
    t = pl.program_id(1)
    ts, c = main_ref.shape
    prev = prev_ref[...]
    nxt = next_ref[...]
    xpad_ref[0:CONV_HALO, :] = jnp.where(t == 0, jnp.zeros_like(prev), prev)
    xpad_ref[CONV_HALO:CONV_HALO + ts, :] = main_ref[...]
    xpad_ref[CONV_HALO + ts:, :] = jnp.where(t == pl.num_programs(1) - 1, jnp.zeros_like(nxt), nxt)
    n_shift = ts + CONV_SHIFT_ROWS

    def chunk(ci, carry):
        lanes = pl.ds(pl.multiple_of(ci * CONV_LANES, CONV_LANES), CONV_LANES)
        for s in range(1, SUBLANES):
            shift_ref[s - 1] = xpad_ref[s:s + n_shift, lanes]
        acc = jnp.zeros((ts, CONV_LANES), F32)
        for k in range(CONV_K):
            a, s = divmod(CONV_FIRST + k, SUBLANES)
            if s == 0:
                xk = xpad_ref[a * SUBLANES:a * SUBLANES + ts, lanes]
            else:
                xk = shift_ref[s - 1, a * SUBLANES:a * SUBLANES + ts, :]
            acc = acc + w_ref[k:k + 1, lanes] * xk
        y_ref[:, lanes] = acc + b_ref[:, lanes]
        return carry

    lax.fori_loop(0, c // CONV_LANES, chunk, 0)

    y = y_ref[...]
    mu = jnp.mean(y, axis=-1, keepdims=True)
    yc = y - mu
    var = jnp.mean(yc * yc, axis=-1, keepdims=True)
    z = (yc * lax.rsqrt(var + EPS)) * lg_ref[...] + lb_ref[...]
    o_ref[...] = (z * jax.nn.sigmoid(z)).astype(o_ref.dtype)


def _conv_module(u, seq, conv_w, conv_b, ln_g, ln_b):
    m, c = u.shape
    ts = _block(seq, 256)
    assert ts % CONV_HALO == 0 and c % CONV_LANES == 0 and m % seq == 0
    nt = seq // ts
    hb = ts // CONV_HALO
    last_h = seq // CONV_HALO - 1

    def prev_map(b, t):
        return (b * (seq // CONV_HALO) + jnp.maximum(t * hb - 1, 0), 0)

    def next_map(b, t):
        return (b * (seq // CONV_HALO) + jnp.minimum((t + 1) * hb, last_h), 0)

    vec = lambda: pl.BlockSpec((1, c), lambda b, t: (0, 0))
    return pl.pallas_call(
        _conv_kernel,
        grid=(m // seq, nt),
        in_specs=[pl.BlockSpec((CONV_HALO, c), prev_map),
                  pl.BlockSpec((ts, c), lambda b, t: (b * nt + t, 0)),
                  pl.BlockSpec((CONV_HALO, c), next_map),
                  pl.BlockSpec((CONV_K, c), lambda b, t: (0, 0)),
                  vec(), vec(), vec()],
        out_specs=pl.BlockSpec((ts, c), lambda b, t: (b * nt + t, 0)),
        out_shape=jax.ShapeDtypeStruct((m, c), BF16),
        scratch_shapes=[pltpu.VMEM((ts + 2 * CONV_HALO, c), F32),
                        pltpu.VMEM((SUBLANES - 1, ts + CONV_SHIFT_ROWS, CONV_LANES), F32),
                        pltpu.VMEM((ts, c), F32)],
        compiler_params=_params(("parallel", "arbitrary")),
        name="conv_module",
    )(u, u, u, conv_w, conv_b.reshape(1, c), ln_g.reshape(1, c), ln_b.reshape(1, c))


N_DR = 2 * NA_ROWS - 1
N_DC = 2 * NA_COLS - 1


def _bias_tile_plan(rows):
    plan = {}
    for v, r0 in enumerate((0, Q_ROWS, rows - Q_ROWS)):
        ks = min(max(r0 - NA_ROWS // 2, 0), rows - KEY_ROWS)
        for g in range(Q_ROWS):
            r = r0 + g
            row_start = min(max(r - NA_ROWS // 2, 0), rows - NA_ROWS)
            for p in range(KEY_ROWS // 2):
                oks = [row_start <= ks + j < row_start + NA_ROWS for j in (2 * p, 2 * p + 1)]
                dr_left = ks + 2 * p - r + NA_ROWS - 1
                key = (dr_left if any(oks) else None, oks[0], oks[1])
                plan.setdefault(key, []).append((v, g, p))
    return plan


def _bias_table_kernel(rpb_ref, o_ref, *, rows):
    base = pl.program_id(0) * (N_DR * N_DC)
    shape = (GRID_W, 2 * GRID_W)
    lane = lax.broadcasted_iota(jnp.int32, shape, 1)
    qc = lax.broadcasted_iota(jnp.int32, shape, 0)
    right = lane >= GRID_W
    kc = jnp.where(right, lane - GRID_W, lane)
    delta = kc - qc + (NA_COLS - 1)
    win_start = jnp.clip(qc - NA_COLS // 2, 0, GRID_W - NA_COLS)
    col_ok = (kc >= win_start) & (kc < win_start + NA_COLS)
    for (dr_left, ok_l, ok_r), dests in _bias_tile_plan(rows).items():
        if dr_left is None:
            tile = jnp.full(shape, MASK_VALUE, F32)
        else:
            acc = jnp.zeros(shape, F32)
            for dc in range(N_DC):
                s_l = rpb_ref[base + dr_left * N_DC + dc] if ok_l else 0.0
                s_r = rpb_ref[base + (dr_left + 1) * N_DC + dc] if ok_r else 0.0
                acc = jnp.where(delta == dc, jnp.where(right, s_r, s_l), acc)
            ok = col_ok if (ok_l and ok_r) else (col_ok & right if ok_r else col_ok & ~right)
            tile = jnp.where(ok, acc, MASK_VALUE)
        for v, g, p in dests:
            o_ref[0, v, g * GRID_W:(g + 1) * GRID_W, p * 2 * GRID_W:(p + 1) * 2 * GRID_W] = tile


def _attn_bias_table(rpb, rows):
    n_heads = rpb.shape[0]
    assert rpb.shape[1:] == (N_DR, N_DC) and rows % Q_ROWS == 0 and rows >= KEY_ROWS + Q_ROWS
    blk = (1, 3, Q_ROWS * GRID_W, KEY_ROWS * GRID_W)
    return pl.pallas_call(
        functools.partial(_bias_table_kernel, rows=rows),
        grid=(n_heads,),
        in_specs=[pl.BlockSpec(memory_space=pltpu.SMEM)],
        out_specs=pl.BlockSpec(blk, lambda h: (h, 0, 0, 0)),
        out_shape=jax.ShapeDtypeStruct((n_heads,) + blk[1:], F32),
        compiler_params=_params(("parallel",)),
        name="bias_table",
    )(rpb.reshape(-1))


def _attn_kernel(q_ref, k_ref, v_ref, tbl_ref, o_ref, s_ref, p_ref, *, rows, heads):
    rb = pl.program_id(2)
    r0 = rb * Q_ROWS
    ks = jnp.clip(r0 - NA_ROWS // 2, 0, rows - KEY_ROWS)
    nk = KEY_ROWS * GRID_W
    keys = pl.ds(pl.multiple_of(ks * GRID_W, GRID_W), nk)
    head_cols = [slice(hd * HEAD_DIM, (hd + 1) * HEAD_DIM) for hd in range(heads)]
    for hd, cols in enumerate(head_cols):
        s = lax.dot_general(q_ref[:, cols], k_ref[keys, cols], (((1,), (1,)), ((), ())),
                            preferred_element_type=F32)
        s_ref[hd] = s + tbl_ref[hd, 0]
    for hd in range(heads):
        s = s_ref[hd]
        p_ref[hd] = jnp.exp(s - jnp.max(s, axis=-1, keepdims=True)).astype(p_ref.dtype)
    ones = jnp.ones((nk, HEAD_DIM), v_ref.dtype)
    for hd, cols in enumerate(head_cols):
        v_ext = jnp.concatenate([v_ref[keys, cols], ones], axis=1)
        o = jnp.dot(p_ref[hd], v_ext, preferred_element_type=F32)
        o_ref[:, cols] = (o[:, :HEAD_DIM] / o[:, HEAD_DIM:]).astype(o_ref.dtype)


def _attention(qk, v, tbl, batch, seq):
    m, c = v.shape
    rows = seq // GRID_W
    nrb = rows // Q_ROWS
    heads = min(ATTN_HEADS_PER_STEP, c // HEAD_DIM)
    assert rows % Q_ROWS == 0 and c % (heads * HEAD_DIM) == 0
    bc = heads * HEAD_DIM
    ngrp = c // bc
    bq = Q_ROWS * GRID_W

    def variant(rb):
        return jnp.where(rb == 0, 0, jnp.where(rb == nrb - 1, 2, 1))

    kern = functools.partial(_attn_kernel, rows=rows, heads=heads)
    return pl.pallas_call(
        kern,
        grid=(batch, ngrp, nrb),
        in_specs=[pl.BlockSpec((bq, bc), lambda b, g, r: (b * nrb + r, g)),
                  pl.BlockSpec((seq, bc), lambda b, g, r: (b, ngrp + g)),
                  pl.BlockSpec((seq, bc), lambda b, g, r: (b, g)),
                  pl.BlockSpec((heads, 1, bq, KEY_ROWS * GRID_W), lambda b, g, r: (g, variant(r), 0, 0))],
        out_specs=pl.BlockSpec((bq, bc), lambda b, g, r: (b * nrb + r, g)),
        out_shape=jax.ShapeDtypeStruct((m, c), BF16),
        scratch_shapes=[pltpu.VMEM((heads, bq, KEY_ROWS * GRID_W), F32),
                        pltpu.VMEM((heads, bq, KEY_ROWS * GRID_W), BF16)],
        compiler_params=_params(("parallel", "parallel", "arbitrary")),
        name="natten",
    )(qk, qk, v, tbl)


def _out_proj_kernel(x_ref, c_ref, a_ref, w_ref, o_ref):
    mixed = jnp.concatenate([c_ref[...], a_ref[...]], axis=1)
    o_ref[...] = x_ref[...] + jnp.dot(mixed, w_ref[...].astype(BF16), preferred_element_type=F32)


def _out_proj(x, conv_out, attn_out, w_out):
    m, d = x.shape
    cc = conv_out.shape[1]
    ca = attn_out.shape[1]
    assert w_out.shape[0] == cc + ca
    bm = _block(m, 1024)
    bn = _block(d, 512)
    in_specs = [pl.BlockSpec((bm, bn), lambda i, j: (i, j)),
                pl.BlockSpec((bm, cc), lambda i, j: (i, 0)),
                pl.BlockSpec((bm, ca), lambda i, j: (i, 0)),
                pl.BlockSpec((cc + ca, bn), lambda i, j: (0, j), pipeline_mode=pl.Buffered(3))]
    out_spec = pl.BlockSpec((bm, bn), lambda i, j: (i, j))

    def pipelined(x_hbm, c_hbm, a_hbm, w_hbm, o_hbm):
        pltpu.emit_pipeline(_out_proj_kernel, grid=(m // bm, d // bn), in_specs=in_specs,
                            out_specs=[out_spec])(x_hbm, c_hbm, a_hbm, w_hbm, o_hbm)

    any_spec = pl.BlockSpec(memory_space=pl.ANY)
    return pl.pallas_call(
        pipelined,
        in_specs=[any_spec] * 4,
        out_specs=any_spec,
        out_shape=jax.ShapeDtypeStruct((m, d), F32),
        compiler_params=pltpu.CompilerParams(vmem_limit_bytes=V7X_VMEM_LIMIT),
        name="out_proj",
    )(x, conv_out, attn_out, w_out)


def kernel(x, g_ffn1, w1_gate, w1_up, w1_down, g_mix, w_in, conv_w, conv_b, conv_ln_g, conv_ln_b, q_norm_g, k_norm_g, rpb, w_out, g_ffn2, w2_gate, w2_up, w2_down, g_final):
    batch, seq, d = x.shape
    depth = g_ffn1.shape[0]
    c_conv = conv_w.shape[2]
    c_attn = w_out.shape[1] - c_conv
    n_heads = c_attn // HEAD_DIM
    rows = seq // GRID_W
    scale = HEAD_DIM ** -0.5

    xs = x.reshape(batch * seq, d)
    for l in range(depth):
        qk_gains = jnp.concatenate([jnp.tile(q_norm_g[l] * scale, n_heads), jnp.tile(k_norm_g[l], n_heads)])
        tbl = _attn_bias_table(rpb[l], rows)

        xs, h2 = _ffn(xs, g_ffn1[l], w1_gate[l], w1_up[l], w1_down[l], g_mix[l], final_norm=False)
        u = _proj_glu(h2, w_in[l], c_conv)
        qk = _proj_qk(h2, w_in[l], qk_gains, 2 * c_conv)
        v = _proj_v(h2, w_in[l], 2 * c_conv + 2 * c_attn, c_attn)
        conv_out = _conv_module(u, seq, conv_w[l], conv_b[l], conv_ln_g[l], conv_ln_b[l])
        attn_out = _attention(qk, v, tbl, batch, seq)
        xs = _out_proj(xs, conv_out, attn_out, w_out[l])

        xs = _ffn(xs, g_ffn2[l], w2_gate[l], w2_up[l], w2_down[l], g_final[l], final_norm=True)
    return xs.reshape(batch, seq, d)
```
